```python
import math
import jax, jax.numpy as jnp
from jax import lax
import numpy as np

D_MODEL = 1024
BATCH = 2
SEQ = 16384
DEPTH = 2

EPS = 1e-6
MIX_DIM = D_MODEL
CONV_DIM = MIX_DIM // 2
CONV_GROUPS = 8
CONV_K = 3
LSTM_HEADS = 4
LSTM_DIM = MIX_DIM - CONV_DIM
LSTM_DH = LSTM_DIM // LSTM_HEADS
CHUNK = 128
OFF_XIN = 0
OFF_B = OFF_XIN + CONV_DIM
OFF_C = OFF_B + CONV_DIM
OFF_Q = OFF_C + CONV_DIM
OFF_K = OFF_Q + LSTM_DIM
OFF_V = OFF_K + LSTM_DIM
OFF_O = OFF_V + LSTM_DIM
OFF_IF = OFF_O + LSTM_DIM
IN_COLS = OFF_IF + 2 * LSTM_HEADS
N_MEM = 256
X_HEADS = 4
X_DH = D_MODEL // X_HEADS
P_HEADS = 8
N_KEYS = 128
N_EXPERTS = N_KEYS * N_KEYS
P_QDIM = 256
P_HALF = P_QDIM // 2
TOPK = 16
PEER_BLOCK = 128

kernel_name = 'hybrid_conv_mlstm_peer_block'


def rms_norm(x, g):
    xf = x.astype(jnp.float32)
    y = xf * lax.rsqrt(jnp.mean(xf * xf, axis=-1, keepdims=True) + EPS)
    return (y * g.astype(jnp.float32)).astype(x.dtype)


def mlstm_chunkwise(q, k, v, i_pre, f_pre):
    B, S, H, DH = q.shape
    NC = S // CHUNK
    f32 = jnp.float32

    def to_chunks(a):
        return a.astype(f32).reshape(B, NC, CHUNK, H, DH).transpose(1, 0, 3, 2, 4)

    def gate_chunks(a):
        return a.astype(f32).reshape(B, NC, CHUNK, H).transpose(1, 0, 3, 2)

    qc = to_chunks(q)
    kc = to_chunks(k) * (DH ** -0.5)
    vc = to_chunks(v)
    ig = gate_chunks(i_pre)
    bcum = jnp.cumsum(jax.nn.log_sigmoid(gate_chunks(f_pre)), axis=-1)
    mask = jnp.tril(jnp.ones((CHUNK, CHUNK), dtype=bool))

    def step(carry, inp):
        C, n, m = carry
        q_, k_, v_, i_, b_ = inp
        D = b_[..., :, None] - b_[..., None, :] + i_[..., None, :]
        D = jnp.where(mask, D, -jnp.inf)
        inter = b_ + m[..., None]
        m_t = jnp.maximum(inter, jnp.max(D, axis=-1))
        Dw = jnp.exp(D - m_t[..., None])
        iw = jnp.exp(inter - m_t)
        s = jnp.einsum('bhtd,bhsd->bhts', q_, k_) * Dw
        num = jnp.einsum('bhts,bhsv->bhtv', s, v_) + iw[..., None] * jnp.einsum('bhvk,bhtk->bhtv', C, q_)
        den = jnp.sum(s, axis=-1) + iw * jnp.einsum('bhtk,bhk->bht', q_, n)
        h = num / jnp.maximum(jnp.abs(den), jnp.exp(-m_t))[..., None]
        bL = b_[..., -1]
        gsum = bL[..., None] - b_ + i_
        m_new = jnp.maximum(bL + m, jnp.max(gsum, axis=-1))
        wdec = jnp.exp(bL + m - m_new)
        ws = jnp.exp(gsum - m_new[..., None])
        C_new = wdec[..., None, None] * C + jnp.einsum('bhsv,bhsk->bhvk', v_ * ws[..., None], k_)
        n_new = wdec[..., None] * n + jnp.einsum('bhs,bhsk->bhk', ws, k_)
        return (C_new, n_new, m_new), h

    init = (jnp.zeros((B, H, DH, DH), f32), jnp.zeros((B, H, DH), f32), jnp.zeros((B, H), f32))
    _, hs = lax.scan(step, init, (qc, kc, vc, ig, bcum))
    return hs.transpose(1, 0, 3, 2, 4).reshape(B, S, H, DH)


def hybrid_mixer(h, w_in, b_if, conv_w, conv_b, g_head, w_out):
    B, S, _ = h.shape
    z = h @ w_in
    xin = z[..., OFF_XIN:OFF_B]
    bg = z[..., OFF_B:OFF_C]
    cg = z[..., OFF_C:OFF_Q]
    q = z[..., OFF_Q:OFF_K].reshape(B, S, LSTM_HEADS, LSTM_DH)
    k = z[..., OFF_K:OFF_V].reshape(B, S, LSTM_HEADS, LSTM_DH)
    v = z[..., OFF_V:OFF_O].reshape(B, S, LSTM_HEADS, LSTM_DH)
    o = jax.nn.sigmoid(z[..., OFF_O:OFF_IF])
    gates = z[..., OFF_IF:] + b_if
    i_pre = gates[..., :LSTM_HEADS]
    f_pre = gates[..., LSTM_HEADS:]

    u = cg * xin
    c = lax.conv_general_dilated(u, conv_w[:, None, :], window_strides=(1,), padding=[(CONV_K - 1, 0)],
                                 dimension_numbers=('NWC', 'WIO', 'NWC'), feature_group_count=CONV_DIM)
    y_conv = bg * (c + conv_b)

    ht = mlstm_chunkwise(q, k, v, i_pre, f_pre).reshape(B, S, LSTM_DIM)
    hl = (o.astype(jnp.float32) * ht).reshape(B, S, LSTM_HEADS, LSTM_DH)
    hl = hl * lax.rsqrt(jnp.mean(hl * hl, axis=-1, keepdims=True) + EPS)
    y_lstm = (hl.reshape(B, S, LSTM_DIM) * g_head.astype(jnp.float32)).astype(h.dtype)

    return jnp.concatenate([y_conv, y_lstm], axis=-1) @ w_out


def memory_cross_attention(h, memn, w_xq, w_xk, w_xv, w_xo):
    B, S, _ = h.shape
    q = (h @ w_xq).reshape(B, S, X_HEADS, X_DH)
    k = (memn @ w_xk).reshape(B, N_MEM, X_HEADS, X_DH)
    v = (memn @ w_xv).reshape(B, N_MEM, X_HEADS, X_DH)
    s = jnp.einsum('bshd,bmhd->bhsm', q, k).astype(jnp.float32) * (X_DH ** -0.5)
    p = jax.nn.softmax(s, axis=-1).astype(h.dtype)
    out = jnp.einsum('bhsm,bmhd->bshd', p, v).reshape(B, S, D_MODEL)
    return out @ w_xo


def peer_ffn(h, w_pq, sub_keys, u_experts, v_experts):
    B, S, D = h.shape
    blocks = h.reshape(-1, PEER_BLOCK, D)

    def block(xb):
        q = (xb @ w_pq).reshape(PEER_BLOCK, P_HEADS, 2, P_HALF)
        sc = jnp.einsum('thpc,hpkc->thpk', q, sub_keys)
        sv, si = lax.top_k(sc, TOPK)
        cand = (sv[..., 0, :, None] + sv[..., 1, None, :]).reshape(PEER_BLOCK, P_HEADS, TOPK * TOPK)
        cid = (si[..., 0, :, None] * N_KEYS + si[..., 1, None, :]).reshape(PEER_BLOCK, P_HEADS, TOPK * TOPK)
        fv, fi = lax.top_k(cand, TOPK)
        eid = jnp.take_along_axis(cid, fi, axis=-1)
        g = jax.nn.softmax(fv.astype(jnp.float32), axis=-1).astype(xb.dtype)
        ue = jnp.take(u_experts, eid, axis=0)
        ve = jnp.take(v_experts, eid, axis=0)
        a = jax.nn.gelu(jnp.einsum('thkd,td->thk', ue, xb), approximate=False)
        return jnp.einsum('thk,thkd->td', g * a, ve)

    return lax.map(block, blocks).reshape(B, S, D)


def setup_inputs(seed: int = 0) -> dict:
    key = jax.random.key(seed)
    ks = jax.random.split(key, 24)

    def nrm(k, shape, scale):
        return jax.random.normal(k, shape, jnp.float32) * scale

    x = nrm(ks[0], (BATCH, SEQ, D_MODEL), 1.0)
    mem = nrm(ks[1], (BATCH, N_MEM, D_MODEL), 1.0)
    g_mix = 1.0 + nrm(ks[2], (DEPTH, D_MODEL), 0.02)
    w_in = nrm(ks[3], (DEPTH, D_MODEL, IN_COLS), D_MODEL ** -0.5)
    b_i = nrm(ks[4], (DEPTH, LSTM_HEADS), 0.1)
    b_f = 3.0 + nrm(ks[5], (DEPTH, LSTM_HEADS), 0.1)
    b_if = jnp.concatenate([b_i, b_f], axis=-1)
    conv_w = nrm(ks[6], (DEPTH, CONV_K, CONV_DIM), CONV_K ** -0.5)
    conv_b = nrm(ks[7], (DEPTH, CONV_DIM), 0.02)
    g_head = 1.0 + nrm(ks[8], (DEPTH, LSTM_DIM), 0.02)
    w_out = nrm(ks[9], (DEPTH, MIX_DIM, D_MODEL), MIX_DIM ** -0.5)
    g_xattn = 1.0 + nrm(ks[10], (DEPTH, D_MODEL), 0.02)
    g_mem = 1.0 + nrm(ks[11], (DEPTH, D_MODEL), 0.02)
    w_xq = nrm(ks[12], (DEPTH, D_MODEL, D_MODEL), D_MODEL ** -0.5)
    w_xk = nrm(ks[13], (DEPTH, D_MODEL, D_MODEL), D_MODEL ** -0.5)
    w_xv = nrm(ks[14], (DEPTH, D_MODEL, D_MODEL), D_MODEL ** -0.5)
    w_xo = nrm(ks[15], (DEPTH, D_MODEL, D_MODEL), D_MODEL ** -0.5)
    g_ffn = 1.0 + nrm(ks[16], (DEPTH, D_MODEL), 0.02)
    w_pq = nrm(ks[17], (DEPTH, D_MODEL, P_HEADS * P_QDIM), D_MODEL ** -0.5)
    sub_keys = nrm(ks[18], (DEPTH, P_HEADS, 2, N_KEYS, P_HALF), P_HALF ** -0.5)
    u_experts = nrm(ks[19], (DEPTH, N_EXPERTS, D_MODEL), D_MODEL ** -0.5)
    v_experts = nrm(ks[20], (DEPTH, N_EXPERTS, D_MODEL), 0.25)
    g_final = 1.0 + nrm(ks[21], (D_MODEL,), 0.02)
    return {'x': x, 'mem': mem, 'g_mix': g_mix, 'w_in': w_in, 'b_if': b_if, 'conv_w': conv_w,
            'conv_b': conv_b, 'g_head': g_head, 'w_out': w_out, 'g_xattn': g_xattn, 'g_mem': g_mem,
            'w_xq': w_xq, 'w_xk': w_xk, 'w_xv': w_xv, 'w_xo': w_xo, 'g_ffn': g_ffn, 'w_pq': w_pq,
            'sub_keys': sub_keys, 'u_experts': u_experts, 'v_experts': v_experts, 'g_final': g_final}


def reference(x, mem, g_mix, w_in, b_if, conv_w, conv_b, g_head, w_out, g_xattn, g_mem,
              w_xq, w_xk, w_xv, w_xo, g_ffn, w_pq, sub_keys, u_experts, v_experts, g_final):
    for l in range(DEPTH):
        h = rms_norm(x, g_mix[l])
        x = x + hybrid_mixer(h, w_in[l], b_if[l], conv_w[l], conv_b[l], g_head[l], w_out[l])
        h = rms_norm(x, g_xattn[l])
        x = x + memory_cross_attention(h, rms_norm(mem, g_mem[l]), w_xq[l], w_xk[l], w_xv[l], w_xo[l])
        h = rms_norm(x, g_ffn[l])
        x = x + peer_ffn(h, w_pq[l], sub_keys[l], u_experts[l], v_experts[l])
    return rms_norm(x, g_final)
```

```python
import functools

import jax
import jax.numpy as jnp
from jax import lax
from jax.experimental import pallas as pl
from jax.experimental.pallas import tpu as pltpu

F32 = jnp.float32
BF16 = jnp.bfloat16

EPS = 1e-6
D_MODEL = 1024
CONV_DIM = 512
CONV_K = 3
LSTM_HEADS = 4
LSTM_DH = 128
LSTM_DIM = LSTM_HEADS * LSTM_DH
CHUNK = 128
N_GATES = 2 * LSTM_HEADS
MAIN_COLS = 3 * CONV_DIM + 4 * LSTM_DIM
OFF_Q = 3 * CONV_DIM
N_MEM = 256
X_HEADS = 4
X_DH = D_MODEL // X_HEADS
P_HEADS = 8
N_KEYS = 128
P_HALF = 128
TOPK = 16
N_SEL = P_HEADS * TOPK
HALF_D = D_MODEL // 2
ROW_SUBLANES = HALF_D // 128

LANES = 128
SUBLANES = 8
VMEM_TABLE_LIMIT = 56 * 1024 * 1024

MIX_TILE = 256
XATT_TILE = 512
ROUTE_TILE = 128
PEER_TILE = 128
NORM_TILE = 512

HI_MASK = 0xFFFF0000


def _rms(x, g):
    return x * lax.rsqrt(jnp.mean(x * x, axis=-1, keepdims=True) + EPS) * g


def _dot(a, b):
    return jnp.dot(a, b, preferred_element_type=F32)


def _dot_nt(a, b):
    return lax.dot_general(a, b, (((1,), (1,)), ((), ())), preferred_element_type=F32)


def _dot_tn(a, b):
    return lax.dot_general(a, b, (((0,), (0,)), ((), ())), preferred_element_type=F32)


def _log_sigmoid(x):
    return jnp.minimum(x, 0.0) - jnp.log(1.0 + jnp.exp(-jnp.abs(x)))


def _mixer_kernel(*refs, has_resid):
    if has_resid:
        x_ref, y_ref = refs[0], refs[1]
        refs = refs[2:]
    else:
        x_ref = refs[0]
        refs = refs[1:]
    (g_ref, w_ref, wgc_ref, wgr_ref, bcol_ref, brow_ref, cw_ref, cb_ref, gh_ref, wo_ref,
     o_ref, ct_ref, m_ref, ubuf_ref) = refs
    ts = x_ref.shape[0]

    @pl.when(pl.program_id(1) == 0)
    def _():
        ct_ref[...] = jnp.zeros_like(ct_ref)
        m_ref[...] = jnp.zeros_like(m_ref)
        ubuf_ref[0:SUBLANES, :] = jnp.zeros((SUBLANES, CONV_DIM), F32)

    x = x_ref[...]
    if has_resid:
        x = x + y_ref[...]
    h = _rms(x, g_ref[...]).astype(BF16)

    xin = _dot(h, w_ref[:, 0:CONV_DIM])
    cg = _dot(h, w_ref[:, 2 * CONV_DIM:3 * CONV_DIM])
    u = cg * xin
    ubuf_ref[SUBLANES:SUBLANES + ts, :] = u
    u1 = ubuf_ref[SUBLANES - 1:SUBLANES - 1 + ts, :]
    u2 = ubuf_ref[SUBLANES - 2:SUBLANES - 2 + ts, :]
    conv = cw_ref[0:1, :] * u2 + cw_ref[1:2, :] * u1 + cw_ref[2:3, :] * u
    ubuf_ref[0:SUBLANES, :] = u[ts - SUBLANES:ts, :]
    bg = _dot(h, w_ref[:, CONV_DIM:2 * CONV_DIM])
    y_conv = bg * (conv + cb_ref[...])

    gcol = _dot(h, wgc_ref[...]) + bcol_ref[...]
    grow = _dot_nt(wgr_ref[...], h) + brow_ref[...]
    lf_col = _log_sigmoid(gcol)
    lf_row = _log_sigmoid(grow)
    r_i = lax.broadcasted_iota(jnp.int32, (CHUNK, CHUNK), 0)
    c_i = lax.broadcasted_iota(jnp.int32, (CHUNK, CHUNK), 1)
    causal = c_i <= r_i
    tri_lo = causal.astype(F32)
    tri_up = (r_i <= c_i).astype(F32)
    ones_col = (lax.broadcasted_iota(jnp.int32, (CHUNK, LSTM_DH), 1) == 0).astype(F32)

    y_chunks = []
    for c in range(ts // CHUNK):
        r0 = c * CHUNK
        gcol_c = gcol[r0:r0 + CHUNK, :]
        grow_c = grow[:, r0:r0 + CHUNK]
        bcol_c = jnp.dot(tri_lo, lf_col[r0:r0 + CHUNK, :], preferred_element_type=F32,
                         precision=lax.Precision.HIGHEST)
        brow_c = jnp.dot(lf_row[:, r0:r0 + CHUNK], tri_up, preferred_element_type=F32,
                         precision=lax.Precision.HIGHEST)
        y_heads = []
        for hd in range(LSTM_HEADS):
            c0 = OFF_Q + hd * LSTM_DH
            q = _dot(h[r0:r0 + CHUNK, :], w_ref[:, c0:c0 + LSTM_DH]).astype(BF16)
            k = (_dot(h[r0:r0 + CHUNK, :], w_ref[:, c0 + LSTM_DIM:c0 + LSTM_DIM + LSTM_DH])
                 * (LSTM_DH ** -0.5)).astype(BF16)
            v = _dot(h[r0:r0 + CHUNK, :], w_ref[:, c0 + 2 * LSTM_DIM:c0 + 2 * LSTM_DIM + LSTM_DH])
            og = _dot(h[r0:r0 + CHUNK, :], w_ref[:, c0 + 3 * LSTM_DIM:c0 + 3 * LSTM_DIM + LSTM_DH])
            i_col = gcol_c[:, hd:hd + 1]
            i_row = grow_c[hd:hd + 1, :]
            b_col = bcol_c[:, LSTM_HEADS + hd:LSTM_HEADS + hd + 1]
            b_row = brow_c[LSTM_HEADS + hd:LSTM_HEADS + hd + 1, :]
            m_prev = m_ref[hd][0:1, 0:1]
            ct = ct_ref[hd]

            dm = jnp.where(causal, b_col - b_row + i_row, -jnp.inf)
            inter = b_col + m_prev
            m_t = jnp.maximum(inter, jnp.max(dm, axis=-1, keepdims=True))
            dw = jnp.exp(dm - m_t)
            iw = jnp.exp(inter - m_t)
            s = _dot_nt(q, k) * dw
            v_aug = jnp.concatenate([v, ones_col], axis=-1)
            num_aug = _dot(s.astype(BF16), v_aug.astype(BF16)) + iw * _dot(q, ct.astype(BF16))
            num = num_aug[:, 0:LSTM_DH]
            den = num_aug[:, LSTM_DH:LSTM_DH + 1]
            ht = num / jnp.maximum(jnp.abs(den), jnp.exp(-m_t))

            b_last = b_col[CHUNK - 1:CHUNK, :]
            gsum = b_last - b_col + i_col
            m_new = jnp.maximum(b_last + m_prev, jnp.max(gsum, axis=0, keepdims=True))
            wdec = jnp.exp(b_last + m_prev - m_new)
            ws = jnp.exp(gsum - m_new)
            ct_ref[hd] = wdec * ct + _dot_tn(k, (v_aug * ws).astype(BF16))
            m_ref[hd] = jnp.broadcast_to(m_new, (SUBLANES, LANES))

            hl = jax.nn.sigmoid(og) * ht
            hl = hl * lax.rsqrt(jnp.mean(hl * hl, axis=-1, keepdims=True) + EPS)
            y_heads.append(hl * gh_ref[:, hd * LSTM_DH:(hd + 1) * LSTM_DH])
        y_chunks.append(jnp.concatenate(y_heads, axis=-1))
    y_lstm = jnp.concatenate(y_chunks, axis=0) if len(y_chunks) > 1 else y_chunks[0]
    y = jnp.concatenate([y_conv, y_lstm], axis=-1).astype(BF16)
    o_ref[...] = x + _dot(y, wo_ref[...])


def _mixer(x2d, resid, batch, seq, p):
    ts = MIX_TILE
    n_s = seq // ts
    tok = lambda b, i: (b * n_s + i, 0)
    const2 = lambda b, i: (0, 0)
    in_specs = [pl.BlockSpec((ts, D_MODEL), tok)]
    args = [x2d]
    if resid is not None:
        in_specs.append(pl.BlockSpec((ts, D_MODEL), tok))
        args.append(resid)
    consts = [p["g_mix"], p["w_main"], p["w_gate_col"], p["w_gate_row"], p["b_col"], p["b_row"],
              p["conv_w"], p["conv_b"], p["g_head"], p["w_out"]]
    for a in consts:
        in_specs.append(pl.BlockSpec(a.shape, const2))
        args.append(a)
    return pl.pallas_call(
        functools.partial(_mixer_kernel, has_resid=resid is not None),
        grid=(batch, n_s),
        in_specs=in_specs,
        out_specs=pl.BlockSpec((ts, D_MODEL), tok),
        out_shape=jax.ShapeDtypeStruct(x2d.shape, F32),
        scratch_shapes=[
            pltpu.VMEM((LSTM_HEADS, LSTM_DH, 2 * LSTM_DH), F32),
            pltpu.VMEM((LSTM_HEADS, SUBLANES, LANES), F32),
            pltpu.VMEM((SUBLANES + ts, CONV_DIM), F32),
        ],
        compiler_params=pltpu.CompilerParams(dimension_semantics=("arbitrary", "arbitrary")),
        name="mixer",
    )(*args)


def _kv_kernel(mem_ref, g_ref, wk_ref, wv_ref, k_ref, v_ref):
    memn = _rms(mem_ref[0], g_ref[...]).astype(BF16)
    k_ref[0] = _dot(memn, wk_ref[...]).astype(BF16)
    v_ref[0] = _dot(memn, wv_ref[...]).astype(BF16)


def _kv(mem, p):
    batch = mem.shape[0]
    blk = pl.BlockSpec((1, N_MEM, D_MODEL), lambda b: (b, 0, 0))
    wspec = pl.BlockSpec((D_MODEL, D_MODEL), lambda b: (0, 0))
    return pl.pallas_call(
        _kv_kernel,
        grid=(batch,),
        in_specs=[blk, pl.BlockSpec((1, D_MODEL), lambda b: (0, 0)), wspec, wspec],
        out_specs=[blk, blk],
        out_shape=[jax.ShapeDtypeStruct(mem.shape, BF16)] * 2,
        name="xattn_kv",
    )(mem, p["g_mem"], p["w_xk"], p["w_xv"])


def _xattn_kernel(x_ref, g_ref, wq_ref, k_ref, v_ref, wo_ref, o_ref):
    x = x_ref[...]
    h = _rms(x, g_ref[...]).astype(BF16)
    q = _dot(h, wq_ref[...])
    outs = []
    for hd in range(X_HEADS):
        sl = slice(hd * X_DH, (hd + 1) * X_DH)
        s = _dot_nt(q[:, sl].astype(BF16), k_ref[0, :, sl]) * (X_DH ** -0.5)
        e = jnp.exp(s - jnp.max(s, axis=-1, keepdims=True))
        pr = e / jnp.sum(e, axis=-1, keepdims=True)
        outs.append(_dot(pr.astype(BF16), v_ref[0, :, sl]))
    o = jnp.concatenate(outs, axis=-1).astype(BF16)
    o_ref[...] = x + _dot(o, wo_ref[...])


def _xattn(x2d, k, v, seq, p):
    ts = XATT_TILE
    per_b = seq // ts
    tok = lambda i: (i, 0)
    wspec = pl.BlockSpec((D_MODEL, D_MODEL), lambda i: (0, 0))
    kvspec = pl.BlockSpec((1, N_MEM, D_MODEL), lambda i: (i // per_b, 0, 0))
    return pl.pallas_call(
        _xattn_kernel,
        grid=(x2d.shape[0] // ts,),
        in_specs=[pl.BlockSpec((ts, D_MODEL), tok), pl.BlockSpec((1, D_MODEL), lambda i: (0, 0)),
                  wspec, kvspec, kvspec, wspec],
        out_specs=pl.BlockSpec((ts, D_MODEL), tok),
        out_shape=jax.ShapeDtypeStruct(x2d.shape, F32),
        name="xattn",
    )(x2d, p["g_xattn"], p["w_xq"], k, v, p["w_xo"])


def _top16_rows(sc, payload=None):
    rows = sc.shape[0]
    iota = lax.broadcasted_iota(jnp.int32, sc.shape, 0)
    vals, picks = [], []
    for _ in range(TOPK):
        m = jnp.max(sc, axis=0, keepdims=True)
        idx = jnp.min(jnp.where(sc == m, iota, rows), axis=0, keepdims=True)
        hit = iota == idx
        vals.append(m)
        if payload is None:
            picks.append(idx)
        else:
            picks.append(jnp.max(jnp.where(hit, payload, -1), axis=0, keepdims=True))
        sc = jnp.where(hit, -jnp.inf, sc)
    return jnp.concatenate(vals, axis=0), jnp.concatenate(picks, axis=0)


def _route_kernel(x_ref, g_ref, wq_ref, sk_ref, hp_ref, eid_ref, gate_ref):
    h = _rms(x_ref[...], g_ref[...])
    hb = h.astype(BF16)
    lo = lax.bitcast_convert_type(hb[:, 0:HALF_D].astype(F32), jnp.uint32) >> 16
    hi = lax.bitcast_convert_type(hb[:, HALF_D:D_MODEL].astype(F32), jnp.uint32) & jnp.uint32(HI_MASK)
    hp_ref[...] = lo | hi
    q_t = _dot_nt(wq_ref[...], hb)
    eids, gates = [], []
    for hd in range(P_HEADS):
        sv, si = [], []
        for half in range(2):
            r0 = (hd * 2 + half) * P_HALF
            sc = _dot(sk_ref[hd, half], q_t[r0:r0 + P_HALF, :].astype(BF16))
            v_, i_ = _top16_rows(sc)
            sv.append(v_)
            si.append(i_)
        cand = jnp.concatenate([sv[0][i:i + 1, :] + sv[1] for i in range(TOPK)], axis=0)
        cid = jnp.concatenate([si[0][i:i + 1, :] * N_KEYS + si[1] for i in range(TOPK)], axis=0)
        fv, eid = _top16_rows(cand, payload=cid)
        e = jnp.exp(fv - fv[0:1, :])
        gates.append(e / jnp.sum(e, axis=0, keepdims=True))
        eids.append(eid)
    eid_t = jnp.concatenate(eids, axis=0)
    gate_ref[0] = jnp.concatenate(gates, axis=0)
    eid_ref[...] = (eid_t * ROW_SUBLANES).astype(F32).T.astype(jnp.int32)


def _route(x2d, p):
    tr = ROUTE_TILE
    nt = x2d.shape[0]
    nb = nt // tr
    return pl.pallas_call(
        _route_kernel,
        grid=(nb,),
        in_specs=[pl.BlockSpec((tr, D_MODEL), lambda i: (i, 0)),
                  pl.BlockSpec((1, D_MODEL), lambda i: (0, 0)),
                  pl.BlockSpec(p["w_pq_t"].shape, lambda i: (0, 0)),
                  pl.BlockSpec(p["sub_keys"].shape, lambda i: (0, 0, 0, 0))],
        out_specs=[pl.BlockSpec((tr, HALF_D), lambda i: (i, 0)),
                   pl.BlockSpec((tr, N_SEL), lambda i: (i, 0)),
                   pl.BlockSpec((1, N_SEL, tr), lambda i: (i, 0, 0))],
        out_shape=[jax.ShapeDtypeStruct((nt, HALF_D), jnp.uint32),
                   jax.ShapeDtypeStruct((nt, N_SEL), jnp.int32),
                   jax.ShapeDtypeStruct((nb, N_SEL, tr), F32)],
        name="peer_route",
    )(x2d, p["g_ffn"], p["w_pq_t"], p["sub_keys"])


def _unpack(word):
    lo = lax.bitcast_convert_type(word << 16, F32)
    hi = lax.bitcast_convert_type(word & jnp.uint32(HI_MASK), F32)
    return lo, hi


def _table_row(tbl_ref, row4):
    return tbl_ref[pl.ds(pl.multiple_of(row4, ROW_SUBLANES), ROW_SUBLANES), :]


_REDUCE8_ORDER = (0, 4, 2, 6, 1, 5, 3, 7)


def _reduce8(ps, sub):
    t = [jnp.concatenate([ps[2 * i], ps[2 * i + 1]], axis=0) for i in range(4)]
    m2 = (sub % 4) < 2
    m1 = (sub % 2) == 0

    def pair(a, b):
        a2 = a + pltpu.roll(a, 6, axis=0)
        b2 = b + pltpu.roll(b, 6, axis=0)
        return jnp.where(m2, a2, pltpu.roll(b2, 2, axis=0))

    c1 = pair(t[0], t[1])
    c2 = pair(t[2], t[3])
    c1 = c1 + pltpu.roll(c1, 7, axis=0)
    c2 = c2 + pltpu.roll(c2, 7, axis=0)
    return jnp.where(m1, c1, pltpu.roll(c2, 1, axis=0))


def _erf_gelu(a):
    return 0.5 * a * (1.0 + lax.erf(a * (2.0 ** -0.5)))


def _uside_kernel(eid_ref, hp_ref, gate_ref, tbl_ref, w_ref, acc_ref):
    tb = hp_ref.shape[0] // ROW_SUBLANES
    sub = lax.broadcasted_iota(jnp.int32, (SUBLANES, LANES), 0)
    lane = lax.broadcasted_iota(jnp.int32, (N_SEL, tb), 1)

    def partials(t):
        h_lo, h_hi = _unpack(_table_row(hp_ref, t * ROW_SUBLANES))
        groups = []
        for g in range(N_SEL // SUBLANES):
            ps = []
            for i in range(SUBLANES):
                row = _table_row(tbl_ref, eid_ref[0, t, g * SUBLANES + _REDUCE8_ORDER[i]])
                u_lo, u_hi = _unpack(row)
                ps.append(u_lo * h_lo + u_hi * h_hi)
            groups.append(_reduce8(ps, sub))
        return jnp.concatenate(groups, axis=0)

    def finish(t, part):
        s = jnp.sum(part, axis=-1, keepdims=True)
        acc_ref[...] = jnp.where(lane == t, s, acc_ref[...])

    def body(t, prev):
        part = partials(t)
        finish(t - 1, prev)
        return part

    last = lax.fori_loop(1, tb, body, partials(0))
    finish(tb - 1, last)
    w_t = gate_ref[0] * _erf_gelu(acc_ref[...])
    w_ref[0] = w_t.T


def _uside(eid3, hp4, gate_t, table):
    nb, tb, _ = eid3.shape
    return pl.pallas_call(
        _uside_kernel,
        grid=(nb,),
        in_specs=[pl.BlockSpec((1, tb, N_SEL), lambda i: (i, 0, 0), memory_space=pltpu.SMEM),
                  pl.BlockSpec((tb * ROW_SUBLANES, LANES), lambda i: (i, 0)),
                  pl.BlockSpec((1, N_SEL, tb), lambda i: (i, 0, 0)),
                  pl.BlockSpec(table.shape, lambda i: (0, 0), pipeline_mode=pl.Buffered(1))],
        out_specs=pl.BlockSpec((1, tb, N_SEL), lambda i: (i, 0, 0)),
        out_shape=jax.ShapeDtypeStruct((nb, tb, N_SEL), F32),
        scratch_shapes=[pltpu.VMEM((N_SEL, tb), F32)],
        compiler_params=pltpu.CompilerParams(vmem_limit_bytes=VMEM_TABLE_LIMIT),
        name="peer_u",
    )(eid3, hp4, gate_t, table)


def _vside_kernel(eid_ref, w_ref, tbl_ref, o_ref):
    tb = o_ref.shape[0]

    def body(t, carry):
        acc_lo = jnp.zeros((ROW_SUBLANES, LANES), F32)
        acc_hi = jnp.zeros((ROW_SUBLANES, LANES), F32)
        for k in range(N_SEL):
            lo, hi = _unpack(_table_row(tbl_ref, eid_ref[0, t, k]))
            w = w_ref[0, t, k]
            acc_lo = acc_lo + w * lo
            acc_hi = acc_hi + w * hi
        o_ref[t, 0:ROW_SUBLANES, :] = acc_lo
        o_ref[t, ROW_SUBLANES:2 * ROW_SUBLANES, :] = acc_hi
        return carry

    lax.fori_loop(0, tb, body, 0)


def _vside(eid3, w3, table):
    nb, tb, _ = eid3.shape
    smem = pl.BlockSpec((1, tb, N_SEL), lambda i: (i, 0, 0), memory_space=pltpu.SMEM)
    return pl.pallas_call(
        _vside_kernel,
        grid=(nb,),
        in_specs=[smem, smem,
                  pl.BlockSpec(table.shape, lambda i: (0, 0), pipeline_mode=pl.Buffered(1))],
        out_specs=pl.BlockSpec((tb, 2 * ROW_SUBLANES, LANES), lambda i: (i, 0, 0)),
        out_shape=jax.ShapeDtypeStruct((nb * tb, 2 * ROW_SUBLANES, LANES), F32),
        compiler_params=pltpu.CompilerParams(vmem_limit_bytes=VMEM_TABLE_LIMIT),
        name="peer_v",
    )(eid3, w3, table)


def _final_kernel(x_ref, y_ref, g_ref, o_ref):
    o_ref[...] = _rms(x_ref[...] + y_ref[...], g_ref[...])


def _final(x2d, y2d, g):
    ts = NORM_TILE
    tok = pl.BlockSpec((ts, D_MODEL), lambda i: (i, 0))
    return pl.pallas_call(
        _final_kernel,
        grid=(x2d.shape[0] // ts,),
        in_specs=[tok, tok, pl.BlockSpec((1, D_MODEL), lambda i: (0, 0))],
        out_specs=tok,
        out_shape=jax.ShapeDtypeStruct(x2d.shape, F32),
        name="final_norm",
    )(x2d, y2d, g)


def _pack_table(tbl):
    tb = tbl.astype(BF16)
    lo = lax.bitcast_convert_type(tb[:, :HALF_D], jnp.uint16).astype(jnp.uint32)
    hi = lax.bitcast_convert_type(tb[:, HALF_D:], jnp.uint16).astype(jnp.uint32)
    return (lo | (hi << 16)).reshape(-1, LANES)


def _layer_params(l, g_mix, w_in, b_if, conv_w, conv_b, g_head, w_out, g_xattn, g_mem,
                  w_xq, w_xk, w_xv, w_xo, g_ffn, w_pq, sub_keys, u_experts, v_experts):
    w_gate = w_in[l][:, MAIN_COLS:]
    row = lambda a: a.reshape(1, -1)
    return {
        "g_mix": row(g_mix[l]),
        "w_main": w_in[l][:, :MAIN_COLS].astype(BF16),
        "w_gate_col": jnp.pad(w_gate, ((0, 0), (0, LANES - N_GATES))).astype(BF16),
        "w_gate_row": w_gate.T.astype(BF16),
        "b_col": jnp.pad(b_if[l], (0, LANES - N_GATES)).reshape(1, LANES),
        "b_row": b_if[l].reshape(N_GATES, 1),
        "conv_w": conv_w[l],
        "conv_b": row(conv_b[l]),
        "g_head": row(g_head[l]),
        "w_out": w_out[l].astype(BF16),
        "g_xattn": row(g_xattn[l]),
        "g_mem": row(g_mem[l]),
        "w_xq": w_xq[l].astype(BF16),
        "w_xk": w_xk[l].astype(BF16),
        "w_xv": w_xv[l].astype(BF16),
        "w_xo": w_xo[l].astype(BF16),
        "g_ffn": row(g_ffn[l]),
        "w_pq_t": w_pq[l].T.astype(BF16),
        "sub_keys": sub_keys[l].astype(BF16),
        "u_table": _pack_table(u_experts[l]),
        "v_table": _pack_table(v_experts[l]),
    }


def kernel(x, mem, g_mix, w_in, b_if, conv_w, conv_b, g_head, w_out, g_xattn, g_mem, w_xq, w_xk, w_xv, w_xo, g_ffn, w_pq, sub_keys, u_experts, v_experts, g_final):
    batch, seq, d = x.shape
    depth = w_in.shape[0]
    nt = batch * seq
    assert d == D_MODEL and seq % XATT_TILE == 0 and seq % MIX_TILE == 0 and nt % PEER_TILE == 0
    x2d = x.reshape(nt, d)
    resid = None
    for l in range(depth):
        p = _layer_params(l, g_mix, w_in, b_if, conv_w, conv_b, g_head, w_out, g_xattn, g_mem,
                          w_xq, w_xk, w_xv, w_xo, g_ffn, w_pq, sub_keys, u_experts, v_experts)
        x2d = _mixer(x2d, resid, batch, seq, p)
        k, v = _kv(mem, p)
        x2d = _xattn(x2d, k, v, seq, p)
        hp, eid, gate_t = _route(x2d, p)
        eid3 = eid.reshape(nt // PEER_TILE, PEER_TILE, N_SEL)
        hp4 = hp.reshape(nt * ROW_SUBLANES, LANES)
        w3 = _uside(eid3, hp4, gate_t, p["u_table"])
        resid = _vside(eid3, w3, p["v_table"]).reshape(nt, d)
    return _final(x2d, resid, g_final.reshape(1, d)).reshape(batch, seq, d)
```

```python
import functools

import jax
import jax.numpy as jnp
from jax import lax
from jax.experimental import pallas as pl
from jax.experimental.pallas import tpu as pltpu

F32 = jnp.float32
BF16 = jnp.bfloat16

EPS = 1e-6
D_MODEL = 1024
CONV_DIM = 512
CONV_K = 3
LSTM_HEADS = 4
LSTM_DH = 128
LSTM_DIM = LSTM_HEADS * LSTM_DH
CHUNK = 128
N_GATES = 2 * LSTM_HEADS
MAIN_COLS = 3 * CONV_DIM + 4 * LSTM_DIM
OFF_Q = 3 * CONV_DIM
N_MEM = 256
X_HEADS = 4
X_DH = D_MODEL // X_HEADS
P_HEADS = 8
N_KEYS = 128
P_HALF = 128
TOPK = 16
N_SEL = P_HEADS * TOPK
HALF_D = D_MODEL // 2
ROW_SUBLANES = HALF_D // 128

LANES = 128
SUBLANES = 8
VMEM_TABLE_LIMIT = 56 * 1024 * 1024

MIX_TILE = 256
XATT_TILE = 512
ROUTE_TILE = 128
ROUTE_HEAD_GROUP = 4
PEER_TILE = 128
NORM_TILE = 512

HI_MASK = 0xFFFF0000


def _rms(x, g):
    return x * lax.rsqrt(jnp.mean(x * x, axis=-1, keepdims=True) + EPS) * g


def _dot(a, b):
    return jnp.dot(a, b, preferred_element_type=F32)


def _dot_nt(a, b):
    return lax.dot_general(a, b, (((1,), (1,)), ((), ())), preferred_element_type=F32)


def _dot_tn(a, b):
    return lax.dot_general(a, b, (((0,), (0,)), ((), ())), preferred_element_type=F32)


def _log_sigmoid(x):
    return jnp.minimum(x, 0.0) - jnp.log(1.0 + jnp.exp(-jnp.abs(x)))


def _mixer_kernel(*refs, has_resid):
    if has_resid:
        x_ref, y_ref = refs[0], refs[1]
        refs = refs[2:]
    else:
        x_ref = refs[0]
        refs = refs[1:]
    (g_ref, w_ref, wgc_ref, wgr_ref, bcol_ref, brow_ref, cw_ref, cb_ref, gh_ref, wo_ref,
     o_ref, ct_ref, m_ref, ubuf_ref) = refs
    ts = x_ref.shape[0]

    @pl.when(pl.program_id(1) == 0)
    def _():
        ct_ref[...] = jnp.zeros_like(ct_ref)
        m_ref[...] = jnp.zeros_like(m_ref)
        ubuf_ref[0:SUBLANES, :] = jnp.zeros((SUBLANES, CONV_DIM), F32)

    x = x_ref[...]
    if has_resid:
        x = x + y_ref[...]
    h = _rms(x, g_ref[...]).astype(BF16)

    xin = _dot(h, w_ref[:, 0:CONV_DIM])
    cg = _dot(h, w_ref[:, 2 * CONV_DIM:3 * CONV_DIM])
    u = cg * xin
    ubuf_ref[SUBLANES:SUBLANES + ts, :] = u
    u1 = ubuf_ref[SUBLANES - 1:SUBLANES - 1 + ts, :]
    u2 = ubuf_ref[SUBLANES - 2:SUBLANES - 2 + ts, :]
    conv = cw_ref[0:1, :] * u2 + cw_ref[1:2, :] * u1 + cw_ref[2:3, :] * u
    ubuf_ref[0:SUBLANES, :] = u[ts - SUBLANES:ts, :]
    bg = _dot(h, w_ref[:, CONV_DIM:2 * CONV_DIM])
    y_conv = bg * (conv + cb_ref[...])

    gcol = _dot(h, wgc_ref[...]) + bcol_ref[...]
    grow = _dot_nt(wgr_ref[...], h) + brow_ref[...]
    lf_col = _log_sigmoid(gcol)
    lf_row = _log_sigmoid(grow)
    r_i = lax.broadcasted_iota(jnp.int32, (CHUNK, CHUNK), 0)
    c_i = lax.broadcasted_iota(jnp.int32, (CHUNK, CHUNK), 1)
    causal = c_i <= r_i
    tri_lo = causal.astype(F32)
    tri_up = (r_i <= c_i).astype(F32)
    ones_col = (lax.broadcasted_iota(jnp.int32, (CHUNK, LSTM_DH), 1) == 0).astype(F32)

    y_chunks = []
    for c in range(ts // CHUNK):
        r0 = c * CHUNK
        gcol_c = gcol[r0:r0 + CHUNK, :]
        grow_c = grow[:, r0:r0 + CHUNK]
        bcol_c = jnp.dot(tri_lo, lf_col[r0:r0 + CHUNK, :], preferred_element_type=F32,
                         precision=lax.Precision.HIGHEST)
        brow_c = jnp.dot(lf_row[:, r0:r0 + CHUNK], tri_up, preferred_element_type=F32,
                         precision=lax.Precision.HIGHEST)
        y_heads = []
        for hd in range(LSTM_HEADS):
            c0 = OFF_Q + hd * LSTM_DH
            q = _dot(h[r0:r0 + CHUNK, :], w_ref[:, c0:c0 + LSTM_DH]).astype(BF16)
            k = (_dot(h[r0:r0 + CHUNK, :], w_ref[:, c0 + LSTM_DIM:c0 + LSTM_DIM + LSTM_DH])
                 * (LSTM_DH ** -0.5)).astype(BF16)
            v = _dot(h[r0:r0 + CHUNK, :], w_ref[:, c0 + 2 * LSTM_DIM:c0 + 2 * LSTM_DIM + LSTM_DH])
            og = _dot(h[r0:r0 + CHUNK, :], w_ref[:, c0 + 3 * LSTM_DIM:c0 + 3 * LSTM_DIM + LSTM_DH])
            i_col = gcol_c[:, hd:hd + 1]
            i_row = grow_c[hd:hd + 1, :]
            b_col = bcol_c[:, LSTM_HEADS + hd:LSTM_HEADS + hd + 1]
            b_row = brow_c[LSTM_HEADS + hd:LSTM_HEADS + hd + 1, :]
            m_prev = m_ref[hd][0:1, 0:1]
            ct = ct_ref[hd]

            dm = jnp.where(causal, b_col - b_row + i_row, -jnp.inf)
            inter = b_col + m_prev
            m_t = jnp.maximum(inter, jnp.max(dm, axis=-1, keepdims=True))
            dw = jnp.exp(dm - m_t)
            iw = jnp.exp(inter - m_t)
            s = _dot_nt(q, k) * dw
            v_aug = jnp.concatenate([v, ones_col], axis=-1)
            num_aug = _dot(s.astype(BF16), v_aug.astype(BF16)) + iw * _dot(q, ct.astype(BF16))
            num = num_aug[:, 0:LSTM_DH]
            den = num_aug[:, LSTM_DH:LSTM_DH + 1]
            ht = num / jnp.maximum(jnp.abs(den), jnp.exp(-m_t))

            b_last = b_col[CHUNK - 1:CHUNK, :]
            gsum = b_last - b_col + i_col
            m_new = jnp.maximum(b_last + m_prev, jnp.max(gsum, axis=0, keepdims=True))
            wdec = jnp.exp(b_last + m_prev - m_new)
            ws = jnp.exp(gsum - m_new)
            ct_ref[hd] = wdec * ct + _dot_tn(k, (v_aug * ws).astype(BF16))
            m_ref[hd] = jnp.broadcast_to(m_new, (SUBLANES, LANES))

            hl = jax.nn.sigmoid(og) * ht
            hl = hl * lax.rsqrt(jnp.mean(hl * hl, axis=-1, keepdims=True) + EPS)
            y_heads.append(hl * gh_ref[:, hd * LSTM_DH:(hd + 1) * LSTM_DH])
        y_chunks.append(jnp.concatenate(y_heads, axis=-1))
    y_lstm = jnp.concatenate(y_chunks, axis=0) if len(y_chunks) > 1 else y_chunks[0]
    y = jnp.concatenate([y_conv, y_lstm], axis=-1).astype(BF16)
    o_ref[...] = x + _dot(y, wo_ref[...])


def _mixer(x2d, resid, batch, seq, p):
    ts = MIX_TILE
    n_s = seq // ts
    tok = lambda b, i: (b * n_s + i, 0)
    const2 = lambda b, i: (0, 0)
    in_specs = [pl.BlockSpec((ts, D_MODEL), tok)]
    args = [x2d]
    if resid is not None:
        in_specs.append(pl.BlockSpec((ts, D_MODEL), tok))
        args.append(resid)
    consts = [p["g_mix"], p["w_main"], p["w_gate_col"], p["w_gate_row"], p["b_col"], p["b_row"],
              p["conv_w"], p["conv_b"], p["g_head"], p["w_out"]]
    for a in consts:
        in_specs.append(pl.BlockSpec(a.shape, const2))
        args.append(a)
    return pl.pallas_call(
        functools.partial(_mixer_kernel, has_resid=resid is not None),
        grid=(batch, n_s),
        in_specs=in_specs,
        out_specs=pl.BlockSpec((ts, D_MODEL), tok),
        out_shape=jax.ShapeDtypeStruct(x2d.shape, F32),
        scratch_shapes=[
            pltpu.VMEM((LSTM_HEADS, LSTM_DH, 2 * LSTM_DH), F32),
            pltpu.VMEM((LSTM_HEADS, SUBLANES, LANES), F32),
            pltpu.VMEM((SUBLANES + ts, CONV_DIM), F32),
        ],
        compiler_params=pltpu.CompilerParams(dimension_semantics=("arbitrary", "arbitrary")),
        name="mixer",
    )(*args)


def _kv_kernel(mem_ref, g_ref, wk_ref, wv_ref, k_ref, v_ref):
    memn = _rms(mem_ref[0], g_ref[...]).astype(BF16)
    k_ref[0] = _dot(memn, wk_ref[...]).astype(BF16)
    v_ref[0] = _dot(memn, wv_ref[...]).astype(BF16)


def _kv(mem, p):
    batch = mem.shape[0]
    blk = pl.BlockSpec((1, N_MEM, D_MODEL), lambda b: (b, 0, 0))
    wspec = pl.BlockSpec((D_MODEL, D_MODEL), lambda b: (0, 0))
    return pl.pallas_call(
        _kv_kernel,
        grid=(batch,),
        in_specs=[blk, pl.BlockSpec((1, D_MODEL), lambda b: (0, 0)), wspec, wspec],
        out_specs=[blk, blk],
        out_shape=[jax.ShapeDtypeStruct(mem.shape, BF16)] * 2,
        name="xattn_kv",
    )(mem, p["g_mem"], p["w_xk"], p["w_xv"])


def _xattn_kernel(x_ref, g_ref, wq_ref, k_ref, v_ref, wo_ref, o_ref):
    x = x_ref[...]
    h = _rms(x, g_ref[...]).astype(BF16)
    q = _dot(h, wq_ref[...])
    outs = []
    for hd in range(X_HEADS):
        sl = slice(hd * X_DH, (hd + 1) * X_DH)
        s = _dot_nt(q[:, sl].astype(BF16), k_ref[0, :, sl]) * (X_DH ** -0.5)
        e = jnp.exp(s - jnp.max(s, axis=-1, keepdims=True))
        pr = e / jnp.sum(e, axis=-1, keepdims=True)
        outs.append(_dot(pr.astype(BF16), v_ref[0, :, sl]))
    o = jnp.concatenate(outs, axis=-1).astype(BF16)
    o_ref[...] = x + _dot(o, wo_ref[...])


def _xattn(x2d, k, v, seq, p):
    ts = XATT_TILE
    per_b = seq // ts
    tok = lambda i: (i, 0)
    wspec = pl.BlockSpec((D_MODEL, D_MODEL), lambda i: (0, 0))
    kvspec = pl.BlockSpec((1, N_MEM, D_MODEL), lambda i: (i // per_b, 0, 0))
    return pl.pallas_call(
        _xattn_kernel,
        grid=(x2d.shape[0] // ts,),
        in_specs=[pl.BlockSpec((ts, D_MODEL), tok), pl.BlockSpec((1, D_MODEL), lambda i: (0, 0)),
                  wspec, kvspec, kvspec, wspec],
        out_specs=pl.BlockSpec((ts, D_MODEL), tok),
        out_shape=jax.ShapeDtypeStruct(x2d.shape, F32),
        name="xattn",
    )(x2d, p["g_xattn"], p["w_xq"], k, v, p["w_xo"])


def _top16_rows(scs, payloads=None):
    scs = list(scs)
    n = len(scs)
    rows = scs[0].shape[0]
    iota = lax.broadcasted_iota(jnp.int32, scs[0].shape, 0).astype(F32)
    vals = [[] for _ in range(n)]
    picks = [[] for _ in range(n)]
    for _ in range(TOPK):
        for a in range(n):
            sc = scs[a]
            m = jnp.max(sc, axis=0, keepdims=True)
            idx = jnp.min(jnp.where(sc == m, iota, float(rows)), axis=0, keepdims=True)
            hit = iota == idx
            vals[a].append(m)
            if payloads is None:
                picks[a].append(idx)
            else:
                picks[a].append(jnp.max(jnp.where(hit, payloads[a], -1.0), axis=0, keepdims=True))
            scs[a] = jnp.where(hit, -jnp.inf, sc)
    return [(jnp.concatenate(vals[a], axis=0), jnp.concatenate(picks[a], axis=0)) for a in range(n)]


_CAND_PAIRS = tuple((i, j) for i in range(TOPK) for j in range(TOPK) if (i + 1) * (j + 1) <= TOPK)
N_CAND = -(-len(_CAND_PAIRS) // SUBLANES) * SUBLANES


def _cand_constants():
    n = len(_CAND_PAIRS)
    ii = jnp.array([p[0] for p in _CAND_PAIRS] + [-1] * (N_CAND - n))
    jj = jnp.array([p[1] for p in _CAND_PAIRS] + [-1] * (N_CAND - n))
    cols = jnp.arange(TOPK)
    sel_i = (ii[:, None] == cols[None, :]).astype(F32)
    sel_j = (jj[:, None] == cols[None, :]).astype(F32)
    pad = jnp.where(jnp.arange(N_CAND) < n, 0.0, -jnp.inf).astype(F32).reshape(N_CAND, 1)
    return sel_i, sel_j, pad


def _pick_rows(sel, a):
    return jnp.dot(sel, a, preferred_element_type=F32, precision=lax.Precision.HIGHEST)


def _route_kernel(x_ref, g_ref, wq_ref, sk_ref, seli_ref, selj_ref, pad_ref,
                  hp_ref, idx_ref, par_ref, gate_ref):
    h = _rms(x_ref[...], g_ref[...])
    hb = h.astype(BF16)
    lo = lax.bitcast_convert_type(hb[:, 0:HALF_D].astype(F32), jnp.uint32) >> 16
    hi = lax.bitcast_convert_type(hb[:, HALF_D:D_MODEL].astype(F32), jnp.uint32) & jnp.uint32(HI_MASK)
    hp_ref[...] = lo | hi
    q_t = _dot_nt(wq_ref[...], hb)
    sel_i, sel_j, pad = seli_ref[...], selj_ref[...], pad_ref[...]
    eids, gates = [], []
    for h0 in range(0, P_HEADS, ROUTE_HEAD_GROUP):
        scs = []
        for hd in range(h0, h0 + ROUTE_HEAD_GROUP):
            for half in range(2):
                r0 = (hd * 2 + half) * P_HALF
                scs.append(_dot(sk_ref[hd, half], q_t[r0:r0 + P_HALF, :].astype(BF16)))
        tops = _top16_rows(scs)
        cands, cids = [], []
        for g in range(ROUTE_HEAD_GROUP):
            (sv0, si0), (sv1, si1) = tops[2 * g], tops[2 * g + 1]
            cands.append(_pick_rows(sel_i, sv0) + _pick_rows(sel_j, sv1) + pad)
            cids.append(_pick_rows(sel_i, si0) * float(N_KEYS) + _pick_rows(sel_j, si1))
        for fv, eid in _top16_rows(cands, payloads=cids):
            e = jnp.exp(fv - fv[0:1, :])
            gates.append(e / jnp.sum(e, axis=0, keepdims=True))
            eids.append(eid)
    eid_t = jnp.concatenate(eids, axis=0)
    gate_ref[0] = jnp.concatenate(gates, axis=0)
    pair = jnp.floor(eid_t * 0.5)
    par_ref[...] = (eid_t - 2.0 * pair).T
    idx_ref[...] = (pair * float(SUBLANES)).T.astype(jnp.int32)


def _route(x2d, p):
    tr = ROUTE_TILE
    nt = x2d.shape[0]
    nb = nt // tr
    sel_i, sel_j, pad = _cand_constants()
    const2 = lambda i: (0, 0)
    return pl.pallas_call(
        _route_kernel,
        grid=(nb,),
        in_specs=[pl.BlockSpec((tr, D_MODEL), lambda i: (i, 0)),
                  pl.BlockSpec((1, D_MODEL), const2),
                  pl.BlockSpec(p["w_pq_t"].shape, const2),
                  pl.BlockSpec(p["sub_keys"].shape, lambda i: (0, 0, 0, 0)),
                  pl.BlockSpec(sel_i.shape, const2), pl.BlockSpec(sel_j.shape, const2),
                  pl.BlockSpec(pad.shape, const2)],
        out_specs=[pl.BlockSpec((tr, HALF_D), lambda i: (i, 0)),
                   pl.BlockSpec((tr, N_SEL), lambda i: (i, 0)),
                   pl.BlockSpec((tr, N_SEL), lambda i: (i, 0)),
                   pl.BlockSpec((1, N_SEL, tr), lambda i: (i, 0, 0))],
        out_shape=[jax.ShapeDtypeStruct((nt, HALF_D), jnp.uint32),
                   jax.ShapeDtypeStruct((nt, N_SEL), jnp.int32),
                   jax.ShapeDtypeStruct((nt, N_SEL), F32),
                   jax.ShapeDtypeStruct((nb, N_SEL, tr), F32)],
        name="peer_route",
    )(x2d, p["g_ffn"], p["w_pq_t"], p["sub_keys"], sel_i, sel_j, pad)


TILE_ROWS = 2 * SUBLANES
EXPERT_ROWS = SUBLANES
STACK_ROWS = N_SEL * TILE_ROWS
CHUNK_TILES = 16
PEER_UNROLL = 4


def _table_tile(tbl_ref, row8):
    return pltpu.bitcast(tbl_ref[pl.ds(pl.multiple_of(row8, SUBLANES), SUBLANES), :], BF16)


def _peer_constants():
    r = jnp.arange(STACK_ROWS)
    c = jnp.arange(2 * N_SEL)
    rep = ((c[:, None] % N_SEL == r[None, :] // TILE_ROWS)
           & (c[:, None] // N_SEL == (r[None, :] % TILE_ROWS) // EXPERT_ROWS)).astype(BF16)
    rc = jnp.arange(CHUNK_TILES * TILE_ROWS)
    tile_rows = (jnp.arange(CHUNK_TILES)[:, None] == rc[None, :] // TILE_ROWS).astype(F32)
    jj = r % EXPERT_ROWS
    d_rows = (jnp.arange(EXPERT_ROWS)[:, None] == ((jj % 2) * ROW_SUBLANES + jj // 2)[None, :]).astype(F32)
    return rep, tile_rows, d_rows


def _erf_gelu(a):
    return 0.5 * a * (1.0 + lax.erf(a * (2.0 ** -0.5)))


def _uside_kernel(idx_ref, hp_ref, par_ref, gate_ref, rep_ref, trow_ref, tbl_ref, w2_ref, pr_ref, acc_ref):
    tb = par_ref.shape[0]
    par = par_ref[...]
    pr_ref[...] = _dot(jnp.concatenate([1.0 - par, par], axis=-1).astype(BF16), rep_ref[...])
    tile_rows = trow_ref[...]
    lane = lax.broadcasted_iota(jnp.int32, (N_SEL, tb), 1)

    def partials(t0):
        hbs, rows = [], []
        for u in range(PEER_UNROLL):
            h4 = hp_ref[pl.ds(pl.multiple_of((t0 + u) * ROW_SUBLANES, ROW_SUBLANES), ROW_SUBLANES), :]
            hbs.append(pltpu.bitcast(jnp.concatenate([h4, h4], axis=0), BF16))
            rows.append(idx_ref.at[0, t0 + u])
        outs = [[] for _ in range(PEER_UNROLL)]
        for c in range(N_SEL // CHUNK_TILES):
            prods = [[] for _ in range(PEER_UNROLL)]
            for q in range(CHUNK_TILES):
                for u in range(PEER_UNROLL):
                    prods[u].append(_table_tile(tbl_ref, rows[u][c * CHUNK_TILES + q]) * hbs[u])
            for u in range(PEER_UNROLL):
                k0 = c * CHUNK_TILES * TILE_ROWS
                pick = (pr_ref[pl.ds(t0 + u, 1), k0:k0 + CHUNK_TILES * TILE_ROWS] * tile_rows).astype(BF16)
                outs[u].append(_dot(pick, jnp.concatenate(prods[u], axis=0)))
        return tuple(jnp.concatenate(o, axis=0) for o in outs)

    def finish(t0, parts):
        for u in range(PEER_UNROLL):
            s = jnp.sum(parts[u], axis=-1, keepdims=True)
            acc_ref[...] = jnp.where(lane == t0 + u, s, acc_ref[...])

    def body(i, prev):
        parts = partials(i * PEER_UNROLL)
        finish((i - 1) * PEER_UNROLL, prev)
        return parts

    last = lax.fori_loop(1, tb // PEER_UNROLL, body, partials(0))
    finish(tb - PEER_UNROLL, last)
    w = (gate_ref[0] * _erf_gelu(acc_ref[...])).T
    w2_ref[0] = jnp.concatenate([w * (1.0 - par), w * par], axis=-1)


def _uside(idx3, hp4, par, gate_t, table, consts):
    nb, tb, _ = idx3.shape
    rep, tile_rows, _ = consts
    const2 = lambda i: (0, 0)
    return pl.pallas_call(
        _uside_kernel,
        grid=(nb,),
        in_specs=[pl.BlockSpec((1, tb, N_SEL), lambda i: (i, 0, 0), memory_space=pltpu.SMEM),
                  pl.BlockSpec((tb * ROW_SUBLANES, LANES), lambda i: (i, 0)),
                  pl.BlockSpec((tb, N_SEL), lambda i: (i, 0)),
                  pl.BlockSpec((1, N_SEL, tb), lambda i: (i, 0, 0)),
                  pl.BlockSpec(rep.shape, const2),
                  pl.BlockSpec(tile_rows.shape, const2),
                  pl.BlockSpec(table.shape, const2, pipeline_mode=pl.Buffered(1))],
        out_specs=pl.BlockSpec((1, tb, 2 * N_SEL), lambda i: (i, 0, 0)),
        out_shape=jax.ShapeDtypeStruct((nb, tb, 2 * N_SEL), F32),
        scratch_shapes=[pltpu.VMEM((tb, STACK_ROWS), F32), pltpu.VMEM((N_SEL, tb), F32)],
        compiler_params=pltpu.CompilerParams(vmem_limit_bytes=VMEM_TABLE_LIMIT),
        name="peer_u",
    )(idx3, hp4, par, gate_t, rep, tile_rows, table)


def _vside_kernel(idx_ref, w2_ref, rep_ref, drow_ref, tbl_ref, o_ref, whi_ref, wlo_ref):
    tb = o_ref.shape[0]
    w2 = w2_ref[0]
    w_hi = w2.astype(BF16)
    w_lo = (w2 - w_hi.astype(F32)).astype(BF16)
    whi_ref[...] = _dot(w_hi, rep_ref[...])
    wlo_ref[...] = _dot(w_lo, rep_ref[...])
    d_rows = drow_ref[...]

    def body(i, carry):
        t0 = i * PEER_UNROLL
        rows = [idx_ref.at[0, t0 + u] for u in range(PEER_UNROLL)]
        accs = [jnp.zeros((2 * EXPERT_ROWS, LANES), F32) for _ in range(PEER_UNROLL)]
        for c in range(N_SEL // CHUNK_TILES):
            k0 = c * CHUNK_TILES * TILE_ROWS
            kr = slice(k0, k0 + CHUNK_TILES * TILE_ROWS)
            tiles = [[] for _ in range(PEER_UNROLL)]
            for q in range(CHUNK_TILES):
                for u in range(PEER_UNROLL):
                    tiles[u].append(_table_tile(tbl_ref, rows[u][c * CHUNK_TILES + q]))
            for u in range(PEER_UNROLL):
                t = t0 + u
                lhs = jnp.concatenate([whi_ref[pl.ds(t, 1), kr] * d_rows[:, kr],
                                       wlo_ref[pl.ds(t, 1), kr] * d_rows[:, kr]], axis=0).astype(BF16)
                accs[u] = accs[u] + _dot(lhs, jnp.concatenate(tiles[u], axis=0))
        for u in range(PEER_UNROLL):
            o_ref[t0 + u] = accs[u][0:EXPERT_ROWS, :] + accs[u][EXPERT_ROWS:2 * EXPERT_ROWS, :]
        return carry

    lax.fori_loop(0, tb // PEER_UNROLL, body, 0)


def _vside(idx3, w2, table, consts):
    nb, tb, _ = idx3.shape
    rep, _, d_rows = consts
    const2 = lambda i: (0, 0)
    return pl.pallas_call(
        _vside_kernel,
        grid=(nb,),
        in_specs=[pl.BlockSpec((1, tb, N_SEL), lambda i: (i, 0, 0), memory_space=pltpu.SMEM),
                  pl.BlockSpec((1, tb, 2 * N_SEL), lambda i: (i, 0, 0)),
                  pl.BlockSpec(rep.shape, const2),
                  pl.BlockSpec(d_rows.shape, const2),
                  pl.BlockSpec(table.shape, const2, pipeline_mode=pl.Buffered(1))],
        out_specs=pl.BlockSpec((tb, EXPERT_ROWS, LANES), lambda i: (i, 0, 0)),
        out_shape=jax.ShapeDtypeStruct((nb * tb, EXPERT_ROWS, LANES), F32),
        scratch_shapes=[pltpu.VMEM((tb, STACK_ROWS), F32), pltpu.VMEM((tb, STACK_ROWS), F32)],
        compiler_params=pltpu.CompilerParams(vmem_limit_bytes=VMEM_TABLE_LIMIT),
        name="peer_v",
    )(idx3, w2, rep, d_rows, table)


def _final_kernel(x_ref, y_ref, g_ref, o_ref):
    o_ref[...] = _rms(x_ref[...] + y_ref[...], g_ref[...])


def _final(x2d, y2d, g):
    ts = NORM_TILE
    tok = pl.BlockSpec((ts, D_MODEL), lambda i: (i, 0))
    return pl.pallas_call(
        _final_kernel,
        grid=(x2d.shape[0] // ts,),
        in_specs=[tok, tok, pl.BlockSpec((1, D_MODEL), lambda i: (0, 0))],
        out_specs=tok,
        out_shape=jax.ShapeDtypeStruct(x2d.shape, F32),
        name="final_norm",
    )(x2d, y2d, g)


def _pack_table(tbl):
    tb = tbl.astype(BF16)
    lo = lax.bitcast_convert_type(tb[:, :HALF_D], jnp.uint16).astype(jnp.uint32)
    hi = lax.bitcast_convert_type(tb[:, HALF_D:], jnp.uint16).astype(jnp.uint32)
    return (lo | (hi << 16)).reshape(-1, LANES)


def _layer_params(l, g_mix, w_in, b_if, conv_w, conv_b, g_head, w_out, g_xattn, g_mem,
                  w_xq, w_xk, w_xv, w_xo, g_ffn, w_pq, sub_keys, u_experts, v_experts):
    w_gate = w_in[l][:, MAIN_COLS:]
    row = lambda a: a.reshape(1, -1)
    return {
        "g_mix": row(g_mix[l]),
        "w_main": w_in[l][:, :MAIN_COLS].astype(BF16),
        "w_gate_col": jnp.pad(w_gate, ((0, 0), (0, LANES - N_GATES))).astype(BF16),
        "w_gate_row": w_gate.T.astype(BF16),
        "b_col": jnp.pad(b_if[l], (0, LANES - N_GATES)).reshape(1, LANES),
        "b_row": b_if[l].reshape(N_GATES, 1),
        "conv_w": conv_w[l],
        "conv_b": row(conv_b[l]),
        "g_head": row(g_head[l]),
        "w_out": w_out[l].astype(BF16),
        "g_xattn": row(g_xattn[l]),
        "g_mem": row(g_mem[l]),
        "w_xq": w_xq[l].astype(BF16),
        "w_xk": w_xk[l].astype(BF16),
        "w_xv": w_xv[l].astype(BF16),
        "w_xo": w_xo[l].astype(BF16),
        "g_ffn": row(g_ffn[l]),
        "w_pq_t": w_pq[l].T.astype(BF16),
        "sub_keys": sub_keys[l].astype(BF16),
        "u_table": _pack_table(u_experts[l]),
        "v_table": _pack_table(v_experts[l]),
    }


def kernel(x, mem, g_mix, w_in, b_if, conv_w, conv_b, g_head, w_out, g_xattn, g_mem, w_xq, w_xk, w_xv, w_xo, g_ffn, w_pq, sub_keys, u_experts, v_experts, g_final):
    batch, seq, d = x.shape
    depth = w_in.shape[0]
    nt = batch * seq
    assert d == D_MODEL and seq % XATT_TILE == 0 and seq % MIX_TILE == 0 and nt % PEER_TILE == 0
    x2d = x.reshape(nt, d)
    resid = None
    consts = _peer_constants()
    for l in range(depth):
        p = _layer_params(l, g_mix, w_in, b_if, conv_w, conv_b, g_head, w_out, g_xattn, g_mem,
                          w_xq, w_xk, w_xv, w_xo, g_ffn, w_pq, sub_keys, u_experts, v_experts)
        x2d = _mixer(x2d, resid, batch, seq, p)
        k, v = _kv(mem, p)
        x2d = _xattn(x2d, k, v, seq, p)
        hp, idx, par, gate_t = _route(x2d, p)
        idx3 = idx.reshape(nt // PEER_TILE, PEER_TILE, N_SEL)
        hp4 = hp.reshape(nt * ROW_SUBLANES, LANES)
        w2 = _uside(idx3, hp4, par, gate_t, p["u_table"], consts)
        resid = _vside(idx3, w2, p["v_table"], consts).reshape(nt, d)
    return _final(x2d, resid, g_final.reshape(1, d)).reshape(batch, seq, d)
```

```python
import functools

import jax
import jax.numpy as jnp
from jax import lax
from jax.experimental import pallas as pl
from jax.experimental.pallas import tpu as pltpu

F32 = jnp.float32
BF16 = jnp.bfloat16

EPS = 1e-6
D_MODEL = 1024
CONV_DIM = 512
CONV_K = 3
LSTM_HEADS = 4
LSTM_DH = 128
LSTM_DIM = LSTM_HEADS * LSTM_DH
CHUNK = 128
N_GATES = 2 * LSTM_HEADS
MAIN_COLS = 3 * CONV_DIM + 4 * LSTM_DIM
OFF_Q = 3 * CONV_DIM
N_MEM = 256
X_HEADS = 4
X_DH = D_MODEL // X_HEADS
P_HEADS = 8
N_KEYS = 128
P_HALF = 128
TOPK = 16
N_SEL = P_HEADS * TOPK
HALF_D = D_MODEL // 2
ROW_SUBLANES = HALF_D // 128

LANES = 128
SUBLANES = 8
VMEM_TABLE_LIMIT = 56 * 1024 * 1024

MIX_TILE = 256
XATT_TILE = 512
ROUTE_TILE = 512
ROUTE_HEAD_GROUP = 4
PEER_TILE = 128
assert ROUTE_TILE % PEER_TILE == 0
NORM_TILE = 512

HI_MASK = 0xFFFF0000


def _rms(x, g):
    return x * lax.rsqrt(jnp.mean(x * x, axis=-1, keepdims=True) + EPS) * g


def _dot(a, b):
    return jnp.dot(a, b, preferred_element_type=F32)


def _dot_nt(a, b):
    return lax.dot_general(a, b, (((1,), (1,)), ((), ())), preferred_element_type=F32)


def _dot_tn(a, b):
    return lax.dot_general(a, b, (((0,), (0,)), ((), ())), preferred_element_type=F32)


def _log_sigmoid(x):
    return jnp.minimum(x, 0.0) - jnp.log(1.0 + jnp.exp(-jnp.abs(x)))


def _mixer_kernel(*refs, has_resid):
    if has_resid:
        x_ref, y_ref = refs[0], refs[1]
        refs = refs[2:]
    else:
        x_ref = refs[0]
        refs = refs[1:]
    (g_ref, w_ref, wgc_ref, wgr_ref, bcol_ref, brow_ref, cw_ref, cb_ref, gh_ref, wo_ref,
     o_ref, ct_ref, m_ref, ubuf_ref) = refs
    ts = x_ref.shape[0]

    @pl.when(pl.program_id(1) == 0)
    def _():
        ct_ref[...] = jnp.zeros_like(ct_ref)
        m_ref[...] = jnp.zeros_like(m_ref)
        ubuf_ref[0:SUBLANES, :] = jnp.zeros((SUBLANES, CONV_DIM), F32)

    x = x_ref[...]
    if has_resid:
        x = x + y_ref[...]
    h = _rms(x, g_ref[...]).astype(BF16)

    xin = _dot(h, w_ref[:, 0:CONV_DIM])
    cg = _dot(h, w_ref[:, 2 * CONV_DIM:3 * CONV_DIM])
    u = cg * xin
    ubuf_ref[SUBLANES:SUBLANES + ts, :] = u
    u1 = ubuf_ref[SUBLANES - 1:SUBLANES - 1 + ts, :]
    u2 = ubuf_ref[SUBLANES - 2:SUBLANES - 2 + ts, :]
    conv = cw_ref[0:1, :] * u2 + cw_ref[1:2, :] * u1 + cw_ref[2:3, :] * u
    ubuf_ref[0:SUBLANES, :] = u[ts - SUBLANES:ts, :]
    bg = _dot(h, w_ref[:, CONV_DIM:2 * CONV_DIM])
    y_conv = bg * (conv + cb_ref[...])

    gcol = _dot(h, wgc_ref[...]) + bcol_ref[...]
    grow = _dot_nt(wgr_ref[...], h) + brow_ref[...]
    lf_col = _log_sigmoid(gcol)
    lf_row = _log_sigmoid(grow)
    r_i = lax.broadcasted_iota(jnp.int32, (CHUNK, CHUNK), 0)
    c_i = lax.broadcasted_iota(jnp.int32, (CHUNK, CHUNK), 1)
    causal = c_i <= r_i
    tri_lo = causal.astype(F32)
    tri_up = (r_i <= c_i).astype(F32)
    ones_col = (lax.broadcasted_iota(jnp.int32, (CHUNK, LSTM_DH), 1) == 0).astype(F32)

    zq = _dot(h, w_ref[:, OFF_Q:OFF_Q + LSTM_DIM])
    zk = _dot(h, w_ref[:, OFF_Q + LSTM_DIM:OFF_Q + 2 * LSTM_DIM])
    zv = _dot(h, w_ref[:, OFF_Q + 2 * LSTM_DIM:OFF_Q + 3 * LSTM_DIM])
    zo = _dot(h, w_ref[:, OFF_Q + 3 * LSTM_DIM:OFF_Q + 4 * LSTM_DIM])

    y_chunks = []
    for c in range(ts // CHUNK):
        r0 = c * CHUNK
        gcol_c = gcol[r0:r0 + CHUNK, :]
        grow_c = grow[:, r0:r0 + CHUNK]
        bcol_c = jnp.dot(tri_lo, lf_col[r0:r0 + CHUNK, :], preferred_element_type=F32,
                         precision=lax.Precision.HIGHEST)
        brow_c = jnp.dot(lf_row[:, r0:r0 + CHUNK], tri_up, preferred_element_type=F32,
                         precision=lax.Precision.HIGHEST)
        y_heads = []
        for hd in range(LSTM_HEADS):
            hs = slice(hd * LSTM_DH, (hd + 1) * LSTM_DH)
            q = zq[r0:r0 + CHUNK, hs].astype(BF16)
            k = (zk[r0:r0 + CHUNK, hs] * (LSTM_DH ** -0.5)).astype(BF16)
            v = zv[r0:r0 + CHUNK, hs]
            og = zo[r0:r0 + CHUNK, hs]
            i_col = gcol_c[:, hd:hd + 1]
            i_row = grow_c[hd:hd + 1, :]
            b_col = bcol_c[:, LSTM_HEADS + hd:LSTM_HEADS + hd + 1]
            b_row = brow_c[LSTM_HEADS + hd:LSTM_HEADS + hd + 1, :]
            m_prev = m_ref[hd][0:1, 0:1]
            ct = ct_ref[hd]

            dm = jnp.where(causal, b_col - b_row + i_row, -jnp.inf)
            inter = b_col + m_prev
            m_t = jnp.maximum(inter, jnp.max(dm, axis=-1, keepdims=True))
            dw = jnp.exp(dm - m_t)
            iw = jnp.exp(inter - m_t)
            s = _dot_nt(q, k) * dw
            v_aug = jnp.concatenate([v, ones_col], axis=-1)
            num_aug = _dot(s.astype(BF16), v_aug.astype(BF16)) + iw * _dot(q, ct.astype(BF16))
            num = num_aug[:, 0:LSTM_DH]
            den = num_aug[:, LSTM_DH:LSTM_DH + 1]
            ht = num / jnp.maximum(jnp.abs(den), jnp.exp(-m_t))

            b_last = b_col[CHUNK - 1:CHUNK, :]
            gsum = b_last - b_col + i_col
            m_new = jnp.maximum(b_last + m_prev, jnp.max(gsum, axis=0, keepdims=True))
            wdec = jnp.exp(b_last + m_prev - m_new)
            ws = jnp.exp(gsum - m_new)
            ct_ref[hd] = wdec * ct + _dot_tn(k, (v_aug * ws).astype(BF16))
            m_ref[hd] = jnp.broadcast_to(m_new, (SUBLANES, LANES))

            hl = jax.nn.sigmoid(og) * ht
            hl = hl * lax.rsqrt(jnp.mean(hl * hl, axis=-1, keepdims=True) + EPS)
            y_heads.append(hl * gh_ref[:, hd * LSTM_DH:(hd + 1) * LSTM_DH])
        y_chunks.append(jnp.concatenate(y_heads, axis=-1))
    y_lstm = jnp.concatenate(y_chunks, axis=0) if len(y_chunks) > 1 else y_chunks[0]
    y = jnp.concatenate([y_conv, y_lstm], axis=-1).astype(BF16)
    o_ref[...] = x + _dot(y, wo_ref[...])


def _mixer(x2d, resid, batch, seq, p):
    ts = MIX_TILE
    n_s = seq // ts
    tok = lambda b, i: (b * n_s + i, 0)
    const2 = lambda b, i: (0, 0)
    in_specs = [pl.BlockSpec((ts, D_MODEL), tok)]
    args = [x2d]
    if resid is not None:
        in_specs.append(pl.BlockSpec((ts, D_MODEL), tok))
        args.append(resid)
    consts = [p["g_mix"], p["w_main"], p["w_gate_col"], p["w_gate_row"], p["b_col"], p["b_row"],
              p["conv_w"], p["conv_b"], p["g_head"], p["w_out"]]
    for a in consts:
        in_specs.append(pl.BlockSpec(a.shape, const2))
        args.append(a)
    return pl.pallas_call(
        functools.partial(_mixer_kernel, has_resid=resid is not None),
        grid=(batch, n_s),
        in_specs=in_specs,
        out_specs=pl.BlockSpec((ts, D_MODEL), tok),
        out_shape=jax.ShapeDtypeStruct(x2d.shape, F32),
        scratch_shapes=[
            pltpu.VMEM((LSTM_HEADS, LSTM_DH, 2 * LSTM_DH), F32),
            pltpu.VMEM((LSTM_HEADS, SUBLANES, LANES), F32),
            pltpu.VMEM((SUBLANES + ts, CONV_DIM), F32),
        ],
        compiler_params=pltpu.CompilerParams(dimension_semantics=("arbitrary", "arbitrary")),
        name="mixer",
    )(*args)


def _kv_kernel(mem_ref, g_ref, wk_ref, wv_ref, k_ref, v_ref):
    memn = _rms(mem_ref[0], g_ref[...]).astype(BF16)
    k_ref[0] = _dot(memn, wk_ref[...]).astype(BF16)
    v_ref[0] = _dot(memn, wv_ref[...]).astype(BF16)


def _kv(mem, p):
    batch = mem.shape[0]
    blk = pl.BlockSpec((1, N_MEM, D_MODEL), lambda b: (b, 0, 0))
    wspec = pl.BlockSpec((D_MODEL, D_MODEL), lambda b: (0, 0))
    return pl.pallas_call(
        _kv_kernel,
        grid=(batch,),
        in_specs=[blk, pl.BlockSpec((1, D_MODEL), lambda b: (0, 0)), wspec, wspec],
        out_specs=[blk, blk],
        out_shape=[jax.ShapeDtypeStruct(mem.shape, BF16)] * 2,
        name="xattn_kv",
    )(mem, p["g_mem"], p["w_xk"], p["w_xv"])


def _xattn_kernel(x_ref, g_ref, wq_ref, k_ref, v_ref, wo_ref, o_ref):
    x = x_ref[...]
    h = _rms(x, g_ref[...]).astype(BF16)
    q = _dot(h, wq_ref[...])
    outs = []
    for hd in range(X_HEADS):
        sl = slice(hd * X_DH, (hd + 1) * X_DH)
        s = _dot_nt(q[:, sl].astype(BF16), k_ref[0, :, sl]) * (X_DH ** -0.5)
        e = jnp.exp(s - jnp.max(s, axis=-1, keepdims=True))
        pr = e / jnp.sum(e, axis=-1, keepdims=True)
        outs.append(_dot(pr.astype(BF16), v_ref[0, :, sl]))
    o = jnp.concatenate(outs, axis=-1).astype(BF16)
    o_ref[...] = x + _dot(o, wo_ref[...])


def _xattn(x2d, k, v, seq, p):
    ts = XATT_TILE
    per_b = seq // ts
    tok = lambda i: (i, 0)
    wspec = pl.BlockSpec((D_MODEL, D_MODEL), lambda i: (0, 0))
    kvspec = pl.BlockSpec((1, N_MEM, D_MODEL), lambda i: (i // per_b, 0, 0))
    return pl.pallas_call(
        _xattn_kernel,
        grid=(x2d.shape[0] // ts,),
        in_specs=[pl.BlockSpec((ts, D_MODEL), tok), pl.BlockSpec((1, D_MODEL), lambda i: (0, 0)),
                  wspec, kvspec, kvspec, wspec],
        out_specs=pl.BlockSpec((ts, D_MODEL), tok),
        out_shape=jax.ShapeDtypeStruct(x2d.shape, F32),
        name="xattn",
    )(x2d, p["g_xattn"], p["w_xq"], k, v, p["w_xo"])


def _top16_rows(scs, payloads=None):
    scs = list(scs)
    n = len(scs)
    rows = scs[0].shape[0]
    iota = lax.broadcasted_iota(jnp.int32, scs[0].shape, 0).astype(F32)
    vals = [[] for _ in range(n)]
    picks = [[] for _ in range(n)]
    for _ in range(TOPK):
        for a in range(n):
            sc = scs[a]
            m = jnp.max(sc, axis=0, keepdims=True)
            idx = jnp.min(jnp.where(sc == m, iota, float(rows)), axis=0, keepdims=True)
            hit = iota == idx
            vals[a].append(m)
            if payloads is None:
                picks[a].append(idx)
            else:
                picks[a].append(jnp.max(jnp.where(hit, payloads[a], -1.0), axis=0, keepdims=True))
            scs[a] = jnp.where(hit, -jnp.inf, sc)
    return [(jnp.concatenate(vals[a], axis=0), jnp.concatenate(picks[a], axis=0)) for a in range(n)]


_CAND_PAIRS = tuple((i, j) for i in range(TOPK) for j in range(TOPK) if (i + 1) * (j + 1) <= TOPK)
N_CAND = -(-len(_CAND_PAIRS) // SUBLANES) * SUBLANES


def _cand_constants():
    n = len(_CAND_PAIRS)
    ii = jnp.array([p[0] for p in _CAND_PAIRS] + [-1] * (N_CAND - n))
    jj = jnp.array([p[1] for p in _CAND_PAIRS] + [-1] * (N_CAND - n))
    cols = jnp.arange(TOPK)
    sel_i = (ii[:, None] == cols[None, :]).astype(F32)
    sel_j = (jj[:, None] == cols[None, :]).astype(F32)
    pad = jnp.where(jnp.arange(N_CAND) < n, 0.0, -jnp.inf).astype(F32).reshape(N_CAND, 1)
    return sel_i, sel_j, pad


def _pick_rows(sel, a):
    return jnp.dot(sel, a, preferred_element_type=F32, precision=lax.Precision.HIGHEST)


def _route_kernel(x_ref, g_ref, wq_ref, sk_ref, seli_ref, selj_ref, pad_ref,
                  hp_ref, idx_ref, par_ref, gate_ref):
    h = _rms(x_ref[...], g_ref[...])
    hb = h.astype(BF16)
    lo = lax.bitcast_convert_type(hb[:, 0:HALF_D].astype(F32), jnp.uint32) >> 16
    hi = lax.bitcast_convert_type(hb[:, HALF_D:D_MODEL].astype(F32), jnp.uint32) & jnp.uint32(HI_MASK)
    hp_ref[...] = lo | hi
    q_t = _dot_nt(wq_ref[...], hb)
    sel_i, sel_j, pad = seli_ref[...], selj_ref[...], pad_ref[...]
    eids, gates = [], []
    for h0 in range(0, P_HEADS, ROUTE_HEAD_GROUP):
        scs = []
        for hd in range(h0, h0 + ROUTE_HEAD_GROUP):
            for half in range(2):
                r0 = (hd * 2 + half) * P_HALF
                scs.append(_dot(sk_ref[hd, half], q_t[r0:r0 + P_HALF, :].astype(BF16)))
        tops = _top16_rows(scs)
        cands, cids = [], []
        for g in range(ROUTE_HEAD_GROUP):
            (sv0, si0), (sv1, si1) = tops[2 * g], tops[2 * g + 1]
            cands.append(_pick_rows(sel_i, sv0) + _pick_rows(sel_j, sv1) + pad)
            cids.append(_pick_rows(sel_i, si0) * float(N_KEYS) + _pick_rows(sel_j, si1))
        for fv, eid in _top16_rows(cands, payloads=cids):
            e = jnp.exp(fv - fv[0:1, :])
            gates.append(e / jnp.sum(e, axis=0, keepdims=True))
            eids.append(eid)
    eid_t = jnp.concatenate(eids, axis=0)
    gate_ref[0] = jnp.concatenate(gates, axis=0)
    pair = jnp.floor(eid_t * 0.5)
    par_ref[...] = (eid_t - 2.0 * pair).T
    idx_ref[...] = (pair * float(SUBLANES)).T.astype(jnp.int32)


def _route(x2d, p):
    tr = ROUTE_TILE
    nt = x2d.shape[0]
    nb = nt // tr
    sel_i, sel_j, pad = _cand_constants()
    const2 = lambda i: (0, 0)
    return pl.pallas_call(
        _route_kernel,
        grid=(nb,),
        in_specs=[pl.BlockSpec((tr, D_MODEL), lambda i: (i, 0)),
                  pl.BlockSpec((1, D_MODEL), const2),
                  pl.BlockSpec(p["w_pq_t"].shape, const2),
                  pl.BlockSpec(p["sub_keys"].shape, lambda i: (0, 0, 0, 0)),
                  pl.BlockSpec(sel_i.shape, const2), pl.BlockSpec(sel_j.shape, const2),
                  pl.BlockSpec(pad.shape, const2)],
        out_specs=[pl.BlockSpec((tr, HALF_D), lambda i: (i, 0)),
                   pl.BlockSpec((tr, N_SEL), lambda i: (i, 0)),
                   pl.BlockSpec((tr, N_SEL), lambda i: (i, 0)),
                   pl.BlockSpec((1, N_SEL, tr), lambda i: (i, 0, 0))],
        out_shape=[jax.ShapeDtypeStruct((nt, HALF_D), jnp.uint32),
                   jax.ShapeDtypeStruct((nt, N_SEL), jnp.int32),
                   jax.ShapeDtypeStruct((nt, N_SEL), F32),
                   jax.ShapeDtypeStruct((nb, N_SEL, tr), F32)],
        name="peer_route",
    )(x2d, p["g_ffn"], p["w_pq_t"], p["sub_keys"], sel_i, sel_j, pad)


TILE_ROWS = 2 * SUBLANES
EXPERT_ROWS = SUBLANES
STACK_ROWS = N_SEL * TILE_ROWS
CHUNK_TILES = 16
PEER_UNROLL = 8


def _table_tile(tbl_ref, row8):
    return pltpu.bitcast(tbl_ref[pl.ds(pl.multiple_of(row8, SUBLANES), SUBLANES), :], BF16)


def _peer_constants():
    r = jnp.arange(STACK_ROWS)
    c = jnp.arange(2 * N_SEL)
    rep = ((c[:, None] % N_SEL == r[None, :] // TILE_ROWS)
           & (c[:, None] // N_SEL == (r[None, :] % TILE_ROWS) // EXPERT_ROWS)).astype(BF16)
    rc = jnp.arange(CHUNK_TILES * TILE_ROWS)
    tile_rows = (jnp.arange(CHUNK_TILES)[:, None] == rc[None, :] // TILE_ROWS).astype(F32)
    jj = r % EXPERT_ROWS
    d_rows = (jnp.arange(EXPERT_ROWS)[:, None] == ((jj % 2) * ROW_SUBLANES + jj // 2)[None, :]).astype(F32)
    return rep, tile_rows, d_rows


def _erf_gelu(a):
    return 0.5 * a * (1.0 + lax.erf(a * (2.0 ** -0.5)))


def _uside_kernel(idx_ref, hp_ref, par_ref, gate_ref, rep_ref, trow_ref, tbl_ref, w2_ref, pr_ref, acc_ref):
    tb = par_ref.shape[0]
    par = par_ref[...]
    pr_ref[...] = _dot(jnp.concatenate([1.0 - par, par], axis=-1).astype(BF16), rep_ref[...])
    tile_rows = trow_ref[...]
    lane = lax.broadcasted_iota(jnp.int32, (N_SEL, tb), 1)

    def partials(t0):
        hbs, rows = [], []
        for u in range(PEER_UNROLL):
            h4 = hp_ref[pl.ds(pl.multiple_of((t0 + u) * ROW_SUBLANES, ROW_SUBLANES), ROW_SUBLANES), :]
            hbs.append(pltpu.bitcast(jnp.concatenate([h4, h4], axis=0), BF16))
            rows.append(idx_ref.at[0, t0 + u])
        outs = [[] for _ in range(PEER_UNROLL)]
        for c in range(N_SEL // CHUNK_TILES):
            prods = [[] for _ in range(PEER_UNROLL)]
            for q in range(CHUNK_TILES):
                for u in range(PEER_UNROLL):
                    prods[u].append(_table_tile(tbl_ref, rows[u][c * CHUNK_TILES + q]) * hbs[u])
            for u in range(PEER_UNROLL):
                k0 = c * CHUNK_TILES * TILE_ROWS
                pick = (pr_ref[pl.ds(t0 + u, 1), k0:k0 + CHUNK_TILES * TILE_ROWS] * tile_rows).astype(BF16)
                outs[u].append(_dot(pick, jnp.concatenate(prods[u], axis=0)))
        return tuple(jnp.concatenate(o, axis=0) for o in outs)

    def finish(t0, parts):
        for u in range(PEER_UNROLL):
            s = jnp.sum(parts[u], axis=-1, keepdims=True)
            acc_ref[...] = jnp.where(lane == t0 + u, s, acc_ref[...])

    def body(i, prev):
        parts = partials(i * PEER_UNROLL)
        finish((i - 1) * PEER_UNROLL, prev)
        return parts

    last = lax.fori_loop(1, tb // PEER_UNROLL, body, partials(0))
    finish(tb - PEER_UNROLL, last)
    w = (gate_ref[0] * _erf_gelu(acc_ref[...])).T
    w2_ref[0] = jnp.concatenate([w * (1.0 - par), w * par], axis=-1)


def _uside(idx3, hp4, par, gate_t, table, consts):
    nb, tb, _ = idx3.shape
    rep, tile_rows, _ = consts
    const2 = lambda i: (0, 0)
    per_route = gate_t.shape[2] // tb
    return pl.pallas_call(
        _uside_kernel,
        grid=(nb,),
        in_specs=[pl.BlockSpec((1, tb, N_SEL), lambda i: (i, 0, 0), memory_space=pltpu.SMEM),
                  pl.BlockSpec((tb * ROW_SUBLANES, LANES), lambda i: (i, 0)),
                  pl.BlockSpec((tb, N_SEL), lambda i: (i, 0)),
                  pl.BlockSpec((1, N_SEL, tb), lambda i: (i // per_route, 0, i % per_route)),
                  pl.BlockSpec(rep.shape, const2),
                  pl.BlockSpec(tile_rows.shape, const2),
                  pl.BlockSpec(table.shape, const2, pipeline_mode=pl.Buffered(1))],
        out_specs=pl.BlockSpec((1, tb, 2 * N_SEL), lambda i: (i, 0, 0)),
        out_shape=jax.ShapeDtypeStruct((nb, tb, 2 * N_SEL), F32),
        scratch_shapes=[pltpu.VMEM((tb, STACK_ROWS), F32), pltpu.VMEM((N_SEL, tb), F32)],
        compiler_params=pltpu.CompilerParams(vmem_limit_bytes=VMEM_TABLE_LIMIT),
        name="peer_u",
    )(idx3, hp4, par, gate_t, rep, tile_rows, table)


def _vside_kernel(idx_ref, w2_ref, rep_ref, drow_ref, tbl_ref, o_ref, whi_ref, wlo_ref):
    tb = o_ref.shape[0]
    w2 = w2_ref[0]
    w_hi = w2.astype(BF16)
    w_lo = (w2 - w_hi.astype(F32)).astype(BF16)
    whi_ref[...] = _dot(w_hi, rep_ref[...])
    wlo_ref[...] = _dot(w_lo, rep_ref[...])
    d_rows = drow_ref[...]

    def body(i, carry):
        t0 = i * PEER_UNROLL
        rows = [idx_ref.at[0, t0 + u] for u in range(PEER_UNROLL)]
        accs = [jnp.zeros((2 * EXPERT_ROWS, LANES), F32) for _ in range(PEER_UNROLL)]
        for c in range(N_SEL // CHUNK_TILES):
            k0 = c * CHUNK_TILES * TILE_ROWS
            kr = slice(k0, k0 + CHUNK_TILES * TILE_ROWS)
            tiles = [[] for _ in range(PEER_UNROLL)]
            for q in range(CHUNK_TILES):
                for u in range(PEER_UNROLL):
                    tiles[u].append(_table_tile(tbl_ref, rows[u][c * CHUNK_TILES + q]))
            for u in range(PEER_UNROLL):
                t = t0 + u
                lhs = jnp.concatenate([whi_ref[pl.ds(t, 1), kr] * d_rows[:, kr],
                                       wlo_ref[pl.ds(t, 1), kr] * d_rows[:, kr]], axis=0).astype(BF16)
                accs[u] = accs[u] + _dot(lhs, jnp.concatenate(tiles[u], axis=0))
        for u in range(PEER_UNROLL):
            o_ref[t0 + u] = accs[u][0:EXPERT_ROWS, :] + accs[u][EXPERT_ROWS:2 * EXPERT_ROWS, :]
        return carry

    lax.fori_loop(0, tb // PEER_UNROLL, body, 0)


def _vside(idx3, w2, table, consts):
    nb, tb, _ = idx3.shape
    rep, _, d_rows = consts
    const2 = lambda i: (0, 0)
    return pl.pallas_call(
        _vside_kernel,
        grid=(nb,),
        in_specs=[pl.BlockSpec((1, tb, N_SEL), lambda i: (i, 0, 0), memory_space=pltpu.SMEM),
                  pl.BlockSpec((1, tb, 2 * N_SEL), lambda i: (i, 0, 0)),
                  pl.BlockSpec(rep.shape, const2),
                  pl.BlockSpec(d_rows.shape, const2),
                  pl.BlockSpec(table.shape, const2, pipeline_mode=pl.Buffered(1))],
        out_specs=pl.BlockSpec((tb, EXPERT_ROWS, LANES), lambda i: (i, 0, 0)),
        out_shape=jax.ShapeDtypeStruct((nb * tb, EXPERT_ROWS, LANES), F32),
        scratch_shapes=[pltpu.VMEM((tb, STACK_ROWS), F32), pltpu.VMEM((tb, STACK_ROWS), F32)],
        compiler_params=pltpu.CompilerParams(vmem_limit_bytes=VMEM_TABLE_LIMIT),
        name="peer_v",
    )(idx3, w2, rep, d_rows, table)


def _final_kernel(x_ref, y_ref, g_ref, o_ref):
    o_ref[...] = _rms(x_ref[...] + y_ref[...], g_ref[...])


def _final(x2d, y2d, g):
    ts = NORM_TILE
    tok = pl.BlockSpec((ts, D_MODEL), lambda i: (i, 0))
    return pl.pallas_call(
        _final_kernel,
        grid=(x2d.shape[0] // ts,),
        in_specs=[tok, tok, pl.BlockSpec((1, D_MODEL), lambda i: (0, 0))],
        out_specs=tok,
        out_shape=jax.ShapeDtypeStruct(x2d.shape, F32),
        name="final_norm",
    )(x2d, y2d, g)


def _pack_table(tbl):
    tb = tbl.astype(BF16)
    lo = lax.bitcast_convert_type(tb[:, :HALF_D], jnp.uint16).astype(jnp.uint32)
    hi = lax.bitcast_convert_type(tb[:, HALF_D:], jnp.uint16).astype(jnp.uint32)
    return (lo | (hi << 16)).reshape(-1, LANES)


def _layer_params(l, g_mix, w_in, b_if, conv_w, conv_b, g_head, w_out, g_xattn, g_mem,
                  w_xq, w_xk, w_xv, w_xo, g_ffn, w_pq, sub_keys, u_experts, v_experts):
    w_gate = w_in[l][:, MAIN_COLS:]
    row = lambda a: a.reshape(1, -1)
    return {
        "g_mix": row(g_mix[l]),
        "w_main": w_in[l][:, :MAIN_COLS].astype(BF16),
        "w_gate_col": jnp.pad(w_gate, ((0, 0), (0, LANES - N_GATES))).astype(BF16),
        "w_gate_row": w_gate.T.astype(BF16),
        "b_col": jnp.pad(b_if[l], (0, LANES - N_GATES)).reshape(1, LANES),
        "b_row": b_if[l].reshape(N_GATES, 1),
        "conv_w": conv_w[l],
        "conv_b": row(conv_b[l]),
        "g_head": row(g_head[l]),
        "w_out": w_out[l].astype(BF16),
        "g_xattn": row(g_xattn[l]),
        "g_mem": row(g_mem[l]),
        "w_xq": w_xq[l].astype(BF16),
        "w_xk": w_xk[l].astype(BF16),
        "w_xv": w_xv[l].astype(BF16),
        "w_xo": w_xo[l].astype(BF16),
        "g_ffn": row(g_ffn[l]),
        "w_pq_t": w_pq[l].T.astype(BF16),
        "sub_keys": sub_keys[l].astype(BF16),
        "u_table": _pack_table(u_experts[l]),
        "v_table": _pack_table(v_experts[l]),
    }


def kernel(x, mem, g_mix, w_in, b_if, conv_w, conv_b, g_head, w_out, g_xattn, g_mem, w_xq, w_xk, w_xv, w_xo, g_ffn, w_pq, sub_keys, u_experts, v_experts, g_final):
    batch, seq, d = x.shape
    depth = w_in.shape[0]
    nt = batch * seq
    assert d == D_MODEL and seq % XATT_TILE == 0 and seq % MIX_TILE == 0 and nt % ROUTE_TILE == 0
    x2d = x.reshape(nt, d)
    resid = None
    consts = _peer_constants()
    for l in range(depth):
        p = _layer_params(l, g_mix, w_in, b_if, conv_w, conv_b, g_head, w_out, g_xattn, g_mem,
                          w_xq, w_xk, w_xv, w_xo, g_ffn, w_pq, sub_keys, u_experts, v_experts)
        x2d = _mixer(x2d, resid, batch, seq, p)
        k, v = _kv(mem, p)
        x2d = _xattn(x2d, k, v, seq, p)
        hp, idx, par, gate_t = _route(x2d, p)
        idx3 = idx.reshape(nt // PEER_TILE, PEER_TILE, N_SEL)
        hp4 = hp.reshape(nt * ROW_SUBLANES, LANES)
        w2 = _uside(idx3, hp4, par, gate_t, p["u_table"], consts)
        resid = _vside(idx3, w2, p["v_table"], consts).reshape(nt, d)
    return _final(x2d, resid, g_final.reshape(1, d)).reshape(batch, seq, d)
```

```python
import functools

import jax
import jax.numpy as jnp
from jax import lax
from jax.experimental import pallas as pl
from jax.experimental.pallas import tpu as pltpu

F32 = jnp.float32
BF16 = jnp.bfloat16

EPS = 1e-6
D_MODEL = 1024
CONV_DIM = 512
CONV_K = 3
LSTM_HEADS = 4
LSTM_DH = 128
LSTM_DIM = LSTM_HEADS * LSTM_DH
CHUNK = 128
N_GATES = 2 * LSTM_HEADS
MAIN_COLS = 3 * CONV_DIM + 4 * LSTM_DIM
OFF_Q = 3 * CONV_DIM
N_MEM = 256
X_HEADS = 4
X_DH = D_MODEL // X_HEADS
P_HEADS = 8
N_KEYS = 128
P_HALF = 128
TOPK = 16
N_SEL = P_HEADS * TOPK
HALF_D = D_MODEL // 2
ROW_SUBLANES = HALF_D // 128

LANES = 128
SUBLANES = 8
VMEM_TABLE_LIMIT = 56 * 1024 * 1024

MIX_TILE = 256
XATT_TILE = 512
ROUTE_TILE = 512
ROUTE_HEAD_GROUP = 4
PEER_TILE = 128
assert ROUTE_TILE % PEER_TILE == 0
NORM_TILE = 512
PACK_TILE = 256

HI_MASK = 0xFFFF0000


def _rms(x, g):
    return x * lax.rsqrt(jnp.mean(x * x, axis=-1, keepdims=True) + EPS) * g


def _dot(a, b):
    return jnp.dot(a, b, preferred_element_type=F32)


def _dot_nt(a, b):
    return lax.dot_general(a, b, (((1,), (1,)), ((), ())), preferred_element_type=F32)


def _dot_tn(a, b):
    return lax.dot_general(a, b, (((0,), (0,)), ((), ())), preferred_element_type=F32)


def _log_sigmoid(x):
    return jnp.minimum(x, 0.0) - jnp.log(1.0 + jnp.exp(-jnp.abs(x)))


def _mixer_kernel(*refs, has_resid):
    if has_resid:
        x_ref, y_ref = refs[0], refs[1]
        refs = refs[2:]
    else:
        x_ref = refs[0]
        refs = refs[1:]
    (g_ref, w_ref, wgc_ref, wgr_ref, bcol_ref, brow_ref, cw_ref, cb_ref, gh_ref, wo_ref,
     o_ref, ct_ref, m_ref, ubuf_ref) = refs
    ts = x_ref.shape[0]

    @pl.when(pl.program_id(1) == 0)
    def _():
        ct_ref[...] = jnp.zeros_like(ct_ref)
        m_ref[...] = jnp.zeros_like(m_ref)
        ubuf_ref[0:SUBLANES, :] = jnp.zeros((SUBLANES, CONV_DIM), F32)

    x = x_ref[...]
    if has_resid:
        x = x + y_ref[...]
    h = _rms(x, g_ref[...]).astype(BF16)

    xin = _dot(h, w_ref[:, 0:CONV_DIM])
    cg = _dot(h, w_ref[:, 2 * CONV_DIM:3 * CONV_DIM])
    u = cg * xin
    ubuf_ref[SUBLANES:SUBLANES + ts, :] = u
    u1 = ubuf_ref[SUBLANES - 1:SUBLANES - 1 + ts, :]
    u2 = ubuf_ref[SUBLANES - 2:SUBLANES - 2 + ts, :]
    conv = cw_ref[0:1, :] * u2 + cw_ref[1:2, :] * u1 + cw_ref[2:3, :] * u
    ubuf_ref[0:SUBLANES, :] = u[ts - SUBLANES:ts, :]
    bg = _dot(h, w_ref[:, CONV_DIM:2 * CONV_DIM])
    y_conv = bg * (conv + cb_ref[...])

    gcol = _dot(h, wgc_ref[...]) + bcol_ref[...]
    grow = _dot_nt(wgr_ref[...], h) + brow_ref[...]
    lf_col = _log_sigmoid(gcol)
    lf_row = _log_sigmoid(grow)
    r_i = lax.broadcasted_iota(jnp.int32, (CHUNK, CHUNK), 0)
    c_i = lax.broadcasted_iota(jnp.int32, (CHUNK, CHUNK), 1)
    causal = c_i <= r_i
    tri_lo = causal.astype(F32)
    tri_up = (r_i <= c_i).astype(F32)
    ones_col = (lax.broadcasted_iota(jnp.int32, (CHUNK, LSTM_DH), 1) == 0).astype(F32)

    zq = _dot(h, w_ref[:, OFF_Q:OFF_Q + LSTM_DIM])
    zk = _dot(h, w_ref[:, OFF_Q + LSTM_DIM:OFF_Q + 2 * LSTM_DIM])
    zv = _dot(h, w_ref[:, OFF_Q + 2 * LSTM_DIM:OFF_Q + 3 * LSTM_DIM])
    zo = _dot(h, w_ref[:, OFF_Q + 3 * LSTM_DIM:OFF_Q + 4 * LSTM_DIM])

    y_chunks = []
    for c in range(ts // CHUNK):
        r0 = c * CHUNK
        gcol_c = gcol[r0:r0 + CHUNK, :]
        grow_c = grow[:, r0:r0 + CHUNK]
        bcol_c = jnp.dot(tri_lo, lf_col[r0:r0 + CHUNK, :], preferred_element_type=F32,
                         precision=lax.Precision.HIGHEST)
        brow_c = jnp.dot(lf_row[:, r0:r0 + CHUNK], tri_up, preferred_element_type=F32,
                         precision=lax.Precision.HIGHEST)
        y_heads = []
        for hd in range(LSTM_HEADS):
            hs = slice(hd * LSTM_DH, (hd + 1) * LSTM_DH)
            q = zq[r0:r0 + CHUNK, hs].astype(BF16)
            k = (zk[r0:r0 + CHUNK, hs] * (LSTM_DH ** -0.5)).astype(BF16)
            v = zv[r0:r0 + CHUNK, hs]
            og = zo[r0:r0 + CHUNK, hs]
            i_col = gcol_c[:, hd:hd + 1]
            i_row = grow_c[hd:hd + 1, :]
            b_col = bcol_c[:, LSTM_HEADS + hd:LSTM_HEADS + hd + 1]
            b_row = brow_c[LSTM_HEADS + hd:LSTM_HEADS + hd + 1, :]
            m_prev = m_ref[hd][0:1, 0:1]
            ct = ct_ref[hd]

            dm = jnp.where(causal, b_col - b_row + i_row, -jnp.inf)
            inter = b_col + m_prev
            m_t = jnp.maximum(inter, jnp.max(dm, axis=-1, keepdims=True))
            dw = jnp.exp(dm - m_t)
            iw = jnp.exp(inter - m_t)
            s = _dot_nt(q, k) * dw
            v_aug = jnp.concatenate([v, ones_col], axis=-1)
            num_aug = _dot(s.astype(BF16), v_aug.astype(BF16)) + iw * _dot(q, ct.astype(BF16))
            num = num_aug[:, 0:LSTM_DH]
            den = num_aug[:, LSTM_DH:LSTM_DH + 1]
            ht = num / jnp.maximum(jnp.abs(den), jnp.exp(-m_t))

            b_last = b_col[CHUNK - 1:CHUNK, :]
            gsum = b_last - b_col + i_col
            m_new = jnp.maximum(b_last + m_prev, jnp.max(gsum, axis=0, keepdims=True))
            wdec = jnp.exp(b_last + m_prev - m_new)
            ws = jnp.exp(gsum - m_new)
            ct_ref[hd] = wdec * ct + _dot_tn(k, (v_aug * ws).astype(BF16))
            m_ref[hd] = jnp.broadcast_to(m_new, (SUBLANES, LANES))

            hl = jax.nn.sigmoid(og) * ht
            hl = hl * lax.rsqrt(jnp.mean(hl * hl, axis=-1, keepdims=True) + EPS)
            y_heads.append(hl * gh_ref[:, hd * LSTM_DH:(hd + 1) * LSTM_DH])
        y_chunks.append(jnp.concatenate(y_heads, axis=-1))
    y_lstm = jnp.concatenate(y_chunks, axis=0) if len(y_chunks) > 1 else y_chunks[0]
    y = jnp.concatenate([y_conv, y_lstm], axis=-1).astype(BF16)
    o_ref[...] = x + _dot(y, wo_ref[...])


def _mixer(x2d, resid, batch, seq, p):
    ts = MIX_TILE
    n_s = seq // ts
    tok = lambda b, i: (b * n_s + i, 0)
    const2 = lambda b, i: (0, 0)
    in_specs = [pl.BlockSpec((ts, D_MODEL), tok)]
    args = [x2d]
    if resid is not None:
        in_specs.append(pl.BlockSpec((ts, D_MODEL), tok))
        args.append(resid)
    consts = [p["g_mix"], p["w_main"], p["w_gate_col"], p["w_gate_row"], p["b_col"], p["b_row"],
              p["conv_w"], p["conv_b"], p["g_head"], p["w_out"]]
    for a in consts:
        in_specs.append(pl.BlockSpec(a.shape, const2))
        args.append(a)
    return pl.pallas_call(
        functools.partial(_mixer_kernel, has_resid=resid is not None),
        grid=(batch, n_s),
        in_specs=in_specs,
        out_specs=pl.BlockSpec((ts, D_MODEL), tok),
        out_shape=jax.ShapeDtypeStruct(x2d.shape, F32),
        scratch_shapes=[
            pltpu.VMEM((LSTM_HEADS, LSTM_DH, 2 * LSTM_DH), F32),
            pltpu.VMEM((LSTM_HEADS, SUBLANES, LANES), F32),
            pltpu.VMEM((SUBLANES + ts, CONV_DIM), F32),
        ],
        compiler_params=pltpu.CompilerParams(dimension_semantics=("arbitrary", "arbitrary")),
        name="mixer",
    )(*args)


def _kv_kernel(mem_ref, g_ref, wk_ref, wv_ref, k_ref, v_ref):
    memn = _rms(mem_ref[0], g_ref[...]).astype(BF16)
    k_ref[0] = _dot(memn, wk_ref[...]).astype(BF16)
    v_ref[0] = _dot(memn, wv_ref[...]).astype(BF16)


def _kv(mem, p):
    batch = mem.shape[0]
    blk = pl.BlockSpec((1, N_MEM, D_MODEL), lambda b: (b, 0, 0))
    wspec = pl.BlockSpec((D_MODEL, D_MODEL), lambda b: (0, 0))
    return pl.pallas_call(
        _kv_kernel,
        grid=(batch,),
        in_specs=[blk, pl.BlockSpec((1, D_MODEL), lambda b: (0, 0)), wspec, wspec],
        out_specs=[blk, blk],
        out_shape=[jax.ShapeDtypeStruct(mem.shape, BF16)] * 2,
        name="xattn_kv",
    )(mem, p["g_mem"], p["w_xk"], p["w_xv"])


def _xattn_kernel(x_ref, g_ref, wq_ref, k_ref, v_ref, wo_ref, o_ref):
    x = x_ref[...]
    h = _rms(x, g_ref[...]).astype(BF16)
    q = _dot(h, wq_ref[...])
    outs = []
    for hd in range(X_HEADS):
        sl = slice(hd * X_DH, (hd + 1) * X_DH)
        s = _dot_nt(q[:, sl].astype(BF16), k_ref[0, :, sl]) * (X_DH ** -0.5)
        e = jnp.exp(s - jnp.max(s, axis=-1, keepdims=True))
        pr = e / jnp.sum(e, axis=-1, keepdims=True)
        outs.append(_dot(pr.astype(BF16), v_ref[0, :, sl]))
    o = jnp.concatenate(outs, axis=-1).astype(BF16)
    o_ref[...] = x + _dot(o, wo_ref[...])


def _xattn(x2d, k, v, seq, p):
    ts = XATT_TILE
    per_b = seq // ts
    tok = lambda i: (i, 0)
    wspec = pl.BlockSpec((D_MODEL, D_MODEL), lambda i: (0, 0))
    kvspec = pl.BlockSpec((1, N_MEM, D_MODEL), lambda i: (i // per_b, 0, 0))
    return pl.pallas_call(
        _xattn_kernel,
        grid=(x2d.shape[0] // ts,),
        in_specs=[pl.BlockSpec((ts, D_MODEL), tok), pl.BlockSpec((1, D_MODEL), lambda i: (0, 0)),
                  wspec, kvspec, kvspec, wspec],
        out_specs=pl.BlockSpec((ts, D_MODEL), tok),
        out_shape=jax.ShapeDtypeStruct(x2d.shape, F32),
        name="xattn",
    )(x2d, p["g_xattn"], p["w_xq"], k, v, p["w_xo"])


def _top16_rows(scs, payloads=None):
    scs = list(scs)
    n = len(scs)
    rows = scs[0].shape[0]
    iota = lax.broadcasted_iota(jnp.int32, scs[0].shape, 0).astype(F32)
    vals = [[] for _ in range(n)]
    picks = [[] for _ in range(n)]
    for _ in range(TOPK):
        for a in range(n):
            sc = scs[a]
            m = jnp.max(sc, axis=0, keepdims=True)
            idx = jnp.min(jnp.where(sc == m, iota, float(rows)), axis=0, keepdims=True)
            hit = iota == idx
            vals[a].append(m)
            if payloads is None:
                picks[a].append(idx)
            else:
                picks[a].append(jnp.max(jnp.where(hit, payloads[a], -1.0), axis=0, keepdims=True))
            scs[a] = jnp.where(hit, -jnp.inf, sc)
    return [(jnp.concatenate(vals[a], axis=0), jnp.concatenate(picks[a], axis=0)) for a in range(n)]


_CAND_PAIRS = tuple((i, j) for i in range(TOPK) for j in range(TOPK) if (i + 1) * (j + 1) <= TOPK)
N_CAND = -(-len(_CAND_PAIRS) // SUBLANES) * SUBLANES


def _cand_constants():
    n = len(_CAND_PAIRS)
    ii = jnp.array([p[0] for p in _CAND_PAIRS] + [-1] * (N_CAND - n))
    jj = jnp.array([p[1] for p in _CAND_PAIRS] + [-1] * (N_CAND - n))
    cols = jnp.arange(TOPK)
    sel_i = (ii[:, None] == cols[None, :]).astype(F32)
    sel_j = (jj[:, None] == cols[None, :]).astype(F32)
    pad = jnp.where(jnp.arange(N_CAND) < n, 0.0, -jnp.inf).astype(F32).reshape(N_CAND, 1)
    return sel_i, sel_j, pad


def _pick_rows(sel, a):
    return jnp.dot(sel, a, preferred_element_type=F32, precision=lax.Precision.HIGHEST)


def _route_kernel(x_ref, g_ref, wq_ref, sk_ref, seli_ref, selj_ref, pad_ref,
                  hp_ref, idx_ref, par_ref, gate_ref):
    h = _rms(x_ref[...], g_ref[...])
    hb = h.astype(BF16)
    _store_rows_as_sublanes(hp_ref, _pack_words(h))
    q_t = _dot_nt(wq_ref[...], hb)
    sel_i, sel_j, pad = seli_ref[...], selj_ref[...], pad_ref[...]
    eids, gates = [], []
    for h0 in range(0, P_HEADS, ROUTE_HEAD_GROUP):
        scs = []
        for hd in range(h0, h0 + ROUTE_HEAD_GROUP):
            for half in range(2):
                r0 = (hd * 2 + half) * P_HALF
                scs.append(_dot(sk_ref[hd, half], q_t[r0:r0 + P_HALF, :].astype(BF16)))
        tops = _top16_rows(scs)
        cands, cids = [], []
        for g in range(ROUTE_HEAD_GROUP):
            (sv0, si0), (sv1, si1) = tops[2 * g], tops[2 * g + 1]
            cands.append(_pick_rows(sel_i, sv0) + _pick_rows(sel_j, sv1) + pad)
            cids.append(_pick_rows(sel_i, si0) * float(N_KEYS) + _pick_rows(sel_j, si1))
        for fv, eid in _top16_rows(cands, payloads=cids):
            e = jnp.exp(fv - fv[0:1, :])
            gates.append(e / jnp.sum(e, axis=0, keepdims=True))
            eids.append(eid)
    eid_t = jnp.concatenate(eids, axis=0)
    gate_ref[0] = jnp.concatenate(gates, axis=0)
    pair = jnp.floor(eid_t * 0.5)
    par_ref[...] = (eid_t - 2.0 * pair).T
    idx_ref[...] = (pair * float(SUBLANES)).T.astype(jnp.int32)


def _route(x2d, p):
    tr = ROUTE_TILE
    nt = x2d.shape[0]
    nb = nt // tr
    sel_i, sel_j, pad = _cand_constants()
    const2 = lambda i: (0, 0)
    return pl.pallas_call(
        _route_kernel,
        grid=(nb,),
        in_specs=[pl.BlockSpec((tr, D_MODEL), lambda i: (i, 0)),
                  pl.BlockSpec((1, D_MODEL), const2),
                  pl.BlockSpec(p["w_pq_t"].shape, const2),
                  pl.BlockSpec(p["sub_keys"].shape, lambda i: (0, 0, 0, 0)),
                  pl.BlockSpec(sel_i.shape, const2), pl.BlockSpec(sel_j.shape, const2),
                  pl.BlockSpec(pad.shape, const2)],
        out_specs=[pl.BlockSpec((tr * ROW_SUBLANES, LANES), lambda i: (i, 0)),
                   pl.BlockSpec((tr, N_SEL), lambda i: (i, 0)),
                   pl.BlockSpec((tr, N_SEL), lambda i: (i, 0)),
                   pl.BlockSpec((1, N_SEL, tr), lambda i: (i, 0, 0))],
        out_shape=[jax.ShapeDtypeStruct((nt * ROW_SUBLANES, LANES), jnp.uint32),
                   jax.ShapeDtypeStruct((nt, N_SEL), jnp.int32),
                   jax.ShapeDtypeStruct((nt, N_SEL), F32),
                   jax.ShapeDtypeStruct((nb, N_SEL, tr), F32)],
        name="peer_route",
    )(x2d, p["g_ffn"], p["w_pq_t"], p["sub_keys"], sel_i, sel_j, pad)


TILE_ROWS = 2 * SUBLANES
EXPERT_ROWS = SUBLANES
STACK_ROWS = N_SEL * TILE_ROWS
CHUNK_TILES = 16
PEER_UNROLL = 16


def _table_tile(tbl_ref, row8):
    return pltpu.bitcast(tbl_ref[pl.ds(pl.multiple_of(row8, SUBLANES), SUBLANES), :], BF16)


def _peer_constants():
    r = jnp.arange(STACK_ROWS)
    c = jnp.arange(2 * N_SEL)
    rep = ((c[:, None] % N_SEL == r[None, :] // TILE_ROWS)
           & (c[:, None] // N_SEL == (r[None, :] % TILE_ROWS) // EXPERT_ROWS)).astype(BF16)
    rc = jnp.arange(CHUNK_TILES * TILE_ROWS)
    tile_rows = (jnp.arange(CHUNK_TILES)[:, None] == rc[None, :] // TILE_ROWS).astype(F32)
    jj = r % EXPERT_ROWS
    d_rows = (jnp.arange(EXPERT_ROWS)[:, None] == ((jj % 2) * ROW_SUBLANES + jj // 2)[None, :]).astype(F32)
    return rep, tile_rows, d_rows


def _erf_gelu(a):
    return 0.5 * a * (1.0 + lax.erf(a * (2.0 ** -0.5)))


def _uside_kernel(idx_ref, hp_ref, par_ref, gate_ref, rep_ref, trow_ref, tbl_ref, w2_ref, pr_ref, acc_ref):
    tb = par_ref.shape[0]
    par = par_ref[...]
    pr_ref[...] = _dot(jnp.concatenate([1.0 - par, par], axis=-1).astype(BF16), rep_ref[...])
    tile_rows = trow_ref[...]
    lane = lax.broadcasted_iota(jnp.int32, (N_SEL, tb), 1)

    def partials(t0):
        hbs, rows = [], []
        for u in range(PEER_UNROLL):
            h4 = hp_ref[pl.ds(pl.multiple_of((t0 + u) * ROW_SUBLANES, ROW_SUBLANES), ROW_SUBLANES), :]
            hbs.append(pltpu.bitcast(jnp.concatenate([h4, h4], axis=0), BF16))
            rows.append(idx_ref.at[0, t0 + u])
        outs = [[] for _ in range(PEER_UNROLL)]
        for c in range(N_SEL // CHUNK_TILES):
            prods = [[] for _ in range(PEER_UNROLL)]
            for q in range(CHUNK_TILES):
                for u in range(PEER_UNROLL):
                    prods[u].append(_table_tile(tbl_ref, rows[u][c * CHUNK_TILES + q]) * hbs[u])
            for u in range(PEER_UNROLL):
                k0 = c * CHUNK_TILES * TILE_ROWS
                pick = (pr_ref[pl.ds(t0 + u, 1), k0:k0 + CHUNK_TILES * TILE_ROWS] * tile_rows).astype(BF16)
                outs[u].append(_dot(pick, jnp.concatenate(prods[u], axis=0)))
        return tuple(jnp.concatenate(o, axis=0) for o in outs)

    def finish(t0, parts):
        for u in range(PEER_UNROLL):
            s = jnp.sum(parts[u], axis=-1, keepdims=True)
            acc_ref[...] = jnp.where(lane == t0 + u, s, acc_ref[...])

    def body(i, prev):
        parts = partials(i * PEER_UNROLL)
        finish((i - 1) * PEER_UNROLL, prev)
        return parts

    last = lax.fori_loop(1, tb // PEER_UNROLL, body, partials(0))
    finish(tb - PEER_UNROLL, last)
    w = (gate_ref[0] * _erf_gelu(acc_ref[...])).T
    w2_ref[0] = jnp.concatenate([w * (1.0 - par), w * par], axis=-1)


def _uside(idx3, hp4, par, gate_t, table, consts):
    nb, tb, _ = idx3.shape
    rep, tile_rows, _ = consts
    const2 = lambda i: (0, 0)
    per_route = gate_t.shape[2] // tb
    return pl.pallas_call(
        _uside_kernel,
        grid=(nb,),
        in_specs=[pl.BlockSpec((1, tb, N_SEL), lambda i: (i, 0, 0), memory_space=pltpu.SMEM),
                  pl.BlockSpec((tb * ROW_SUBLANES, LANES), lambda i: (i, 0)),
                  pl.BlockSpec((tb, N_SEL), lambda i: (i, 0)),
                  pl.BlockSpec((1, N_SEL, tb), lambda i: (i // per_route, 0, i % per_route)),
                  pl.BlockSpec(rep.shape, const2),
                  pl.BlockSpec(tile_rows.shape, const2),
                  pl.BlockSpec(table.shape, const2, pipeline_mode=pl.Buffered(1))],
        out_specs=pl.BlockSpec((1, tb, 2 * N_SEL), lambda i: (i, 0, 0)),
        out_shape=jax.ShapeDtypeStruct((nb, tb, 2 * N_SEL), F32),
        scratch_shapes=[pltpu.VMEM((tb, STACK_ROWS), F32), pltpu.VMEM((N_SEL, tb), F32)],
        compiler_params=pltpu.CompilerParams(vmem_limit_bytes=VMEM_TABLE_LIMIT),
        name="peer_u",
    )(idx3, hp4, par, gate_t, rep, tile_rows, table)


def _vside_kernel(idx_ref, w2_ref, rep_ref, drow_ref, tbl_ref, o_ref, whi_ref, wlo_ref):
    tb = o_ref.shape[0]
    w2 = w2_ref[0]
    w_hi = w2.astype(BF16)
    w_lo = (w2 - w_hi.astype(F32)).astype(BF16)
    whi_ref[...] = _dot(w_hi, rep_ref[...])
    wlo_ref[...] = _dot(w_lo, rep_ref[...])
    d_rows = drow_ref[...]

    def body(i, carry):
        t0 = i * PEER_UNROLL
        rows = [idx_ref.at[0, t0 + u] for u in range(PEER_UNROLL)]
        accs = [jnp.zeros((2 * EXPERT_ROWS, LANES), F32) for _ in range(PEER_UNROLL)]
        for c in range(N_SEL // CHUNK_TILES):
            k0 = c * CHUNK_TILES * TILE_ROWS
            kr = slice(k0, k0 + CHUNK_TILES * TILE_ROWS)
            tiles = [[] for _ in range(PEER_UNROLL)]
            for q in range(CHUNK_TILES):
                for u in range(PEER_UNROLL):
                    tiles[u].append(_table_tile(tbl_ref, rows[u][c * CHUNK_TILES + q]))
            for u in range(PEER_UNROLL):
                t = t0 + u
                lhs = jnp.concatenate([whi_ref[pl.ds(t, 1), kr] * d_rows[:, kr],
                                       wlo_ref[pl.ds(t, 1), kr] * d_rows[:, kr]], axis=0).astype(BF16)
                accs[u] = accs[u] + _dot(lhs, jnp.concatenate(tiles[u], axis=0))
        for u in range(PEER_UNROLL):
            o_ref[t0 + u] = accs[u][0:EXPERT_ROWS, :] + accs[u][EXPERT_ROWS:2 * EXPERT_ROWS, :]
        return carry

    lax.fori_loop(0, tb // PEER_UNROLL, body, 0)


def _vside(idx3, w2, table, consts):
    nb, tb, _ = idx3.shape
    rep, _, d_rows = consts
    const2 = lambda i: (0, 0)
    return pl.pallas_call(
        _vside_kernel,
        grid=(nb,),
        in_specs=[pl.BlockSpec((1, tb, N_SEL), lambda i: (i, 0, 0), memory_space=pltpu.SMEM),
                  pl.BlockSpec((1, tb, 2 * N_SEL), lambda i: (i, 0, 0)),
                  pl.BlockSpec(rep.shape, const2),
                  pl.BlockSpec(d_rows.shape, const2),
                  pl.BlockSpec(table.shape, const2, pipeline_mode=pl.Buffered(1))],
        out_specs=pl.BlockSpec((tb, EXPERT_ROWS, LANES), lambda i: (i, 0, 0)),
        out_shape=jax.ShapeDtypeStruct((nb * tb, EXPERT_ROWS, LANES), F32),
        scratch_shapes=[pltpu.VMEM((tb, STACK_ROWS), F32), pltpu.VMEM((tb, STACK_ROWS), F32)],
        compiler_params=pltpu.CompilerParams(vmem_limit_bytes=VMEM_TABLE_LIMIT),
        name="peer_v",
    )(idx3, w2, rep, d_rows, table)


def _final_kernel(x_ref, y_ref, g_ref, o_ref):
    o_ref[...] = _rms(x_ref[...] + y_ref[...], g_ref[...])


def _final(x2d, y2d, g):
    ts = NORM_TILE
    tok = pl.BlockSpec((ts, D_MODEL), lambda i: (i, 0))
    return pl.pallas_call(
        _final_kernel,
        grid=(x2d.shape[0] // ts,),
        in_specs=[tok, tok, pl.BlockSpec((1, D_MODEL), lambda i: (0, 0))],
        out_specs=tok,
        out_shape=jax.ShapeDtypeStruct(x2d.shape, F32),
        name="final_norm",
    )(x2d, y2d, g)


def _pack_words(x):
    xb = x.astype(BF16)
    lo = lax.bitcast_convert_type(xb[:, 0:HALF_D].astype(F32), jnp.uint32) >> 16
    hi = lax.bitcast_convert_type(xb[:, HALF_D:D_MODEL].astype(F32), jnp.uint32) & jnp.uint32(HI_MASK)
    return lo | hi


def _store_rows_as_sublanes(o_ref, words):
    t = words.shape[0]
    for s in range(ROW_SUBLANES):
        o_ref[pl.ds(s, t, stride=ROW_SUBLANES), :] = words[:, s * LANES:(s + 1) * LANES]


def _pack_kernel(x_ref, o_ref):
    _store_rows_as_sublanes(o_ref, _pack_words(x_ref[...]))


def _pack_table(tbl):
    n = tbl.shape[0]
    return pl.pallas_call(
        _pack_kernel,
        grid=(n // PACK_TILE,),
        in_specs=[pl.BlockSpec((PACK_TILE, D_MODEL), lambda i: (i, 0))],
        out_specs=pl.BlockSpec((PACK_TILE * ROW_SUBLANES, LANES), lambda i: (i, 0)),
        out_shape=jax.ShapeDtypeStruct((n * ROW_SUBLANES, LANES), jnp.uint32),
        name="pack_table",
    )(tbl)


def _layer_params(l, g_mix, w_in, b_if, conv_w, conv_b, g_head, w_out, g_xattn, g_mem,
                  w_xq, w_xk, w_xv, w_xo, g_ffn, w_pq, sub_keys, u_experts, v_experts):
    w_gate = w_in[l][:, MAIN_COLS:]
    row = lambda a: a.reshape(1, -1)
    return {
        "g_mix": row(g_mix[l]),
        "w_main": w_in[l][:, :MAIN_COLS].astype(BF16),
        "w_gate_col": jnp.pad(w_gate, ((0, 0), (0, LANES - N_GATES))).astype(BF16),
        "w_gate_row": w_gate.T.astype(BF16),
        "b_col": jnp.pad(b_if[l], (0, LANES - N_GATES)).reshape(1, LANES),
        "b_row": b_if[l].reshape(N_GATES, 1),
        "conv_w": conv_w[l],
        "conv_b": row(conv_b[l]),
        "g_head": row(g_head[l]),
        "w_out": w_out[l].astype(BF16),
        "g_xattn": row(g_xattn[l]),
        "g_mem": row(g_mem[l]),
        "w_xq": w_xq[l].astype(BF16),
        "w_xk": w_xk[l].astype(BF16),
        "w_xv": w_xv[l].astype(BF16),
        "w_xo": w_xo[l].astype(BF16),
        "g_ffn": row(g_ffn[l]),
        "w_pq_t": w_pq[l].T.astype(BF16),
        "sub_keys": sub_keys[l].astype(BF16),
        "u_table": _pack_table(u_experts[l]),
        "v_table": _pack_table(v_experts[l]),
    }


def kernel(x, mem, g_mix, w_in, b_if, conv_w, conv_b, g_head, w_out, g_xattn, g_mem, w_xq, w_xk, w_xv, w_xo, g_ffn, w_pq, sub_keys, u_experts, v_experts, g_final):
    batch, seq, d = x.shape
    depth = w_in.shape[0]
    nt = batch * seq
    assert d == D_MODEL and seq % XATT_TILE == 0 and seq % MIX_TILE == 0 and nt % ROUTE_TILE == 0
    x2d = x.reshape(nt, d)
    resid = None
    consts = _peer_constants()
    for l in range(depth):
        p = _layer_params(l, g_mix, w_in, b_if, conv_w, conv_b, g_head, w_out, g_xattn, g_mem,
                          w_xq, w_xk, w_xv, w_xo, g_ffn, w_pq, sub_keys, u_experts, v_experts)
        x2d = _mixer(x2d, resid, batch, seq, p)
        k, v = _kv(mem, p)
        x2d = _xattn(x2d, k, v, seq, p)
        hp4, idx, par, gate_t = _route(x2d, p)
        idx3 = idx.reshape(nt // PEER_TILE, PEER_TILE, N_SEL)
        w2 = _uside(idx3, hp4, par, gate_t, p["u_table"], consts)
        resid = _vside(idx3, w2, p["v_table"], consts).reshape(nt, d)
    return _final(x2d, resid, g_final.reshape(1, d)).reshape(batch, seq, d)
```

```python
import functools

import jax
import jax.numpy as jnp
from jax import lax
from jax.experimental import pallas as pl
from jax.experimental.pallas import tpu as pltpu

F32 = jnp.float32
BF16 = jnp.bfloat16

EPS = 1e-6
D_MODEL = 1024
CONV_DIM = 512
CONV_K = 3
LSTM_HEADS = 4
LSTM_DH = 128
LSTM_DIM = LSTM_HEADS * LSTM_DH
CHUNK = 128
N_GATES = 2 * LSTM_HEADS
MAIN_COLS = 3 * CONV_DIM + 4 * LSTM_DIM
OFF_Q = 3 * CONV_DIM
N_MEM = 256
X_HEADS = 4
X_DH = D_MODEL // X_HEADS
P_HEADS = 8
N_KEYS = 128
P_HALF = 128
TOPK = 16
N_SEL = P_HEADS * TOPK
HALF_D = D_MODEL // 2
ROW_SUBLANES = HALF_D // 128

LANES = 128
SUBLANES = 8
VMEM_TABLE_LIMIT = 56 * 1024 * 1024

MIX_TILE = 256
XATT_TILE = 512
ROUTE_TILE = 512
ROUTE_HEAD_GROUP = 4
PEER_TILE = 128
assert ROUTE_TILE % PEER_TILE == 0
NORM_TILE = 512
PACK_TILE = 256

HI_MASK = 0xFFFF0000


def _rms(x, g):
    return x * lax.rsqrt(jnp.mean(x * x, axis=-1, keepdims=True) + EPS) * g


def _dot(a, b):
    return jnp.dot(a, b, preferred_element_type=F32)


def _dot_nt(a, b):
    return lax.dot_general(a, b, (((1,), (1,)), ((), ())), preferred_element_type=F32)


def _dot_tn(a, b):
    return lax.dot_general(a, b, (((0,), (0,)), ((), ())), preferred_element_type=F32)


def _bf16_parts(a):
    hi = a.astype(BF16)
    r = a - hi.astype(F32)
    mid = r.astype(BF16)
    lo = (r - mid.astype(F32)).astype(BF16)
    return hi, mid, lo


def _dot_f32_by_01(a, b, b_is_01):
    if b_is_01:
        return sum(_dot(p, b.astype(BF16)) for p in _bf16_parts(a))
    return sum(_dot(a.astype(BF16), p) for p in _bf16_parts(b))


def _log_sigmoid(x):
    return jnp.minimum(x, 0.0) - jnp.log(1.0 + jnp.exp(-jnp.abs(x)))


def _mixer_kernel(*refs, has_resid):
    if has_resid:
        x_ref, y_ref = refs[0], refs[1]
        refs = refs[2:]
    else:
        x_ref = refs[0]
        refs = refs[1:]
    (g_ref, w_ref, wgc_ref, wgr_ref, bcol_ref, brow_ref, cw_ref, cb_ref, gh_ref, wo_ref,
     o_ref, ct_ref, m_ref, ubuf_ref) = refs
    ts = x_ref.shape[0]

    @pl.when(pl.program_id(1) == 0)
    def _():
        ct_ref[...] = jnp.zeros_like(ct_ref)
        m_ref[...] = jnp.zeros_like(m_ref)
        ubuf_ref[0:SUBLANES, :] = jnp.zeros((SUBLANES, CONV_DIM), F32)

    x = x_ref[...]
    if has_resid:
        x = x + y_ref[...]
    h = _rms(x, g_ref[...]).astype(BF16)

    xin = _dot(h, w_ref[:, 0:CONV_DIM])
    cg = _dot(h, w_ref[:, 2 * CONV_DIM:3 * CONV_DIM])
    u = cg * xin
    ubuf_ref[SUBLANES:SUBLANES + ts, :] = u
    u1 = ubuf_ref[SUBLANES - 1:SUBLANES - 1 + ts, :]
    u2 = ubuf_ref[SUBLANES - 2:SUBLANES - 2 + ts, :]
    conv = cw_ref[0:1, :] * u2 + cw_ref[1:2, :] * u1 + cw_ref[2:3, :] * u
    ubuf_ref[0:SUBLANES, :] = u[ts - SUBLANES:ts, :]
    bg = _dot(h, w_ref[:, CONV_DIM:2 * CONV_DIM])
    y_conv = bg * (conv + cb_ref[...])

    gcol = _dot(h, wgc_ref[...]) + bcol_ref[...]
    grow = _dot_nt(wgr_ref[...], h) + brow_ref[...]
    lf_col = _log_sigmoid(gcol)
    lf_row = _log_sigmoid(grow)
    r_i = lax.broadcasted_iota(jnp.int32, (CHUNK, CHUNK), 0)
    c_i = lax.broadcasted_iota(jnp.int32, (CHUNK, CHUNK), 1)
    causal = c_i <= r_i
    tri_lo = causal.astype(F32)
    tri_up = (r_i <= c_i).astype(F32)
    ones_col = (lax.broadcasted_iota(jnp.int32, (CHUNK, LSTM_DH), 1) == 0).astype(F32)

    zq = _dot(h, w_ref[:, OFF_Q:OFF_Q + LSTM_DIM])
    zk = _dot(h, w_ref[:, OFF_Q + LSTM_DIM:OFF_Q + 2 * LSTM_DIM])
    zv = _dot(h, w_ref[:, OFF_Q + 2 * LSTM_DIM:OFF_Q + 3 * LSTM_DIM])
    zo = _dot(h, w_ref[:, OFF_Q + 3 * LSTM_DIM:OFF_Q + 4 * LSTM_DIM])

    y_chunks = []
    for c in range(ts // CHUNK):
        r0 = c * CHUNK
        gcol_c = gcol[r0:r0 + CHUNK, :]
        grow_c = grow[:, r0:r0 + CHUNK]
        bcol_c = _dot_f32_by_01(tri_lo, lf_col[r0:r0 + CHUNK, :], False)
        brow_c = _dot_f32_by_01(lf_row[:, r0:r0 + CHUNK], tri_up, True)
        y_heads = []
        for hd in range(LSTM_HEADS):
            hs = slice(hd * LSTM_DH, (hd + 1) * LSTM_DH)
            q = zq[r0:r0 + CHUNK, hs].astype(BF16)
            k = (zk[r0:r0 + CHUNK, hs] * (LSTM_DH ** -0.5)).astype(BF16)
            v = zv[r0:r0 + CHUNK, hs]
            og = zo[r0:r0 + CHUNK, hs]
            i_col = gcol_c[:, hd:hd + 1]
            i_row = grow_c[hd:hd + 1, :]
            b_col = bcol_c[:, LSTM_HEADS + hd:LSTM_HEADS + hd + 1]
            b_row = brow_c[LSTM_HEADS + hd:LSTM_HEADS + hd + 1, :]
            m_prev = m_ref[hd][0:1, 0:1]
            ct = ct_ref[hd]

            dm = jnp.where(causal, b_col - b_row + i_row, -jnp.inf)
            inter = b_col + m_prev
            m_t = jnp.maximum(inter, jnp.max(dm, axis=-1, keepdims=True))
            dw = jnp.exp(dm - m_t)
            iw = jnp.exp(inter - m_t)
            s = _dot_nt(q, k) * dw
            v_aug = jnp.concatenate([v, ones_col], axis=-1)
            num_aug = _dot(s.astype(BF16), v_aug.astype(BF16)) + iw * _dot(q, ct.astype(BF16))
            num = num_aug[:, 0:LSTM_DH]
            den = num_aug[:, LSTM_DH:LSTM_DH + 1]
            ht = num / jnp.maximum(jnp.abs(den), jnp.exp(-m_t))

            b_last = b_col[CHUNK - 1:CHUNK, :]
            gsum = b_last - b_col + i_col
            m_new = jnp.maximum(b_last + m_prev, jnp.max(gsum, axis=0, keepdims=True))
            wdec = jnp.exp(b_last + m_prev - m_new)
            ws = jnp.exp(gsum - m_new)
            ct_ref[hd] = wdec * ct + _dot_tn(k, (v_aug * ws).astype(BF16))
            m_ref[hd] = jnp.broadcast_to(m_new, (SUBLANES, LANES))

            hl = jax.nn.sigmoid(og) * ht
            hl = hl * lax.rsqrt(jnp.mean(hl * hl, axis=-1, keepdims=True) + EPS)
            y_heads.append(hl * gh_ref[:, hd * LSTM_DH:(hd + 1) * LSTM_DH])
        y_chunks.append(jnp.concatenate(y_heads, axis=-1))
    y_lstm = jnp.concatenate(y_chunks, axis=0) if len(y_chunks) > 1 else y_chunks[0]
    y = jnp.concatenate([y_conv, y_lstm], axis=-1).astype(BF16)
    o_ref[...] = x + _dot(y, wo_ref[...])


def _mixer(x2d, resid, batch, seq, p):
    ts = MIX_TILE
    n_s = seq // ts
    tok = lambda b, i: (b * n_s + i, 0)
    const2 = lambda b, i: (0, 0)
    in_specs = [pl.BlockSpec((ts, D_MODEL), tok)]
    args = [x2d]
    if resid is not None:
        in_specs.append(pl.BlockSpec((ts, D_MODEL), tok))
        args.append(resid)
    consts = [p["g_mix"], p["w_main"], p["w_gate_col"], p["w_gate_row"], p["b_col"], p["b_row"],
              p["conv_w"], p["conv_b"], p["g_head"], p["w_out"]]
    for a in consts:
        in_specs.append(pl.BlockSpec(a.shape, const2))
        args.append(a)
    return pl.pallas_call(
        functools.partial(_mixer_kernel, has_resid=resid is not None),
        grid=(batch, n_s),
        in_specs=in_specs,
        out_specs=pl.BlockSpec((ts, D_MODEL), tok),
        out_shape=jax.ShapeDtypeStruct(x2d.shape, F32),
        scratch_shapes=[
            pltpu.VMEM((LSTM_HEADS, LSTM_DH, 2 * LSTM_DH), F32),
            pltpu.VMEM((LSTM_HEADS, SUBLANES, LANES), F32),
            pltpu.VMEM((SUBLANES + ts, CONV_DIM), F32),
        ],
        compiler_params=pltpu.CompilerParams(dimension_semantics=("arbitrary", "arbitrary")),
        name="mixer",
    )(*args)


def _kv_kernel(mem_ref, g_ref, wk_ref, wv_ref, k_ref, v_ref):
    memn = _rms(mem_ref[0], g_ref[...]).astype(BF16)
    k_ref[0] = _dot(memn, wk_ref[...]).astype(BF16)
    v_ref[0] = _dot(memn, wv_ref[...]).astype(BF16)


def _kv(mem, p):
    batch = mem.shape[0]
    blk = pl.BlockSpec((1, N_MEM, D_MODEL), lambda b: (b, 0, 0))
    wspec = pl.BlockSpec((D_MODEL, D_MODEL), lambda b: (0, 0))
    return pl.pallas_call(
        _kv_kernel,
        grid=(batch,),
        in_specs=[blk, pl.BlockSpec((1, D_MODEL), lambda b: (0, 0)), wspec, wspec],
        out_specs=[blk, blk],
        out_shape=[jax.ShapeDtypeStruct(mem.shape, BF16)] * 2,
        name="xattn_kv",
    )(mem, p["g_mem"], p["w_xk"], p["w_xv"])


def _xattn_kernel(x_ref, g_ref, wq_ref, k_ref, v_ref, wo_ref, o_ref):
    x = x_ref[...]
    h = _rms(x, g_ref[...]).astype(BF16)
    q = _dot(h, wq_ref[...])
    outs = []
    for hd in range(X_HEADS):
        sl = slice(hd * X_DH, (hd + 1) * X_DH)
        s = _dot_nt(q[:, sl].astype(BF16), k_ref[0, :, sl]) * (X_DH ** -0.5)
        e = jnp.exp(s - jnp.max(s, axis=-1, keepdims=True))
        pr = e / jnp.sum(e, axis=-1, keepdims=True)
        outs.append(_dot(pr.astype(BF16), v_ref[0, :, sl]))
    o = jnp.concatenate(outs, axis=-1).astype(BF16)
    o_ref[...] = x + _dot(o, wo_ref[...])


def _xattn(x2d, k, v, seq, p):
    ts = XATT_TILE
    per_b = seq // ts
    tok = lambda i: (i, 0)
    wspec = pl.BlockSpec((D_MODEL, D_MODEL), lambda i: (0, 0))
    kvspec = pl.BlockSpec((1, N_MEM, D_MODEL), lambda i: (i // per_b, 0, 0))
    return pl.pallas_call(
        _xattn_kernel,
        grid=(x2d.shape[0] // ts,),
        in_specs=[pl.BlockSpec((ts, D_MODEL), tok), pl.BlockSpec((1, D_MODEL), lambda i: (0, 0)),
                  wspec, kvspec, kvspec, wspec],
        out_specs=pl.BlockSpec((ts, D_MODEL), tok),
        out_shape=jax.ShapeDtypeStruct(x2d.shape, F32),
        name="xattn",
    )(x2d, p["g_xattn"], p["w_xq"], k, v, p["w_xo"])


def _top16_rows(scs, payloads=None):
    scs = list(scs)
    n = len(scs)
    rows = scs[0].shape[0]
    iota = lax.broadcasted_iota(jnp.int32, scs[0].shape, 0).astype(F32)
    vals = [[] for _ in range(n)]
    picks = [[] for _ in range(n)]
    for _ in range(TOPK):
        for a in range(n):
            sc = scs[a]
            m = jnp.max(sc, axis=0, keepdims=True)
            idx = jnp.min(jnp.where(sc == m, iota, float(rows)), axis=0, keepdims=True)
            hit = iota == idx
            vals[a].append(m)
            if payloads is None:
                picks[a].append(idx)
            else:
                picks[a].append(jnp.max(jnp.where(hit, payloads[a], -1.0), axis=0, keepdims=True))
            scs[a] = jnp.where(hit, -jnp.inf, sc)
    return [(jnp.concatenate(vals[a], axis=0), jnp.concatenate(picks[a], axis=0)) for a in range(n)]


def _oddeven_merge_sort_pairs(n):
    pairs = []
    p = 1
    while p < n:
        k = p
        while k >= 1:
            for j in range(k % p, n - k, 2 * k):
                for i in range(min(k, n - j - k)):
                    if (i + j) // (2 * p) == (i + j + k) // (2 * p):
                        pairs.append((i + j, i + j + k))
            k //= 2
        p *= 2
    return tuple(pairs)


KEY_GROUPS = N_KEYS // SUBLANES
_SORT_NETWORK = _oddeven_merge_sort_pairs(KEY_GROUPS)


def _top16_keys(scs):
    n = len(scs)
    t = scs[0].shape[1]
    base = lax.broadcasted_iota(jnp.int32, (SUBLANES, t), 0).astype(F32)
    vs = [[sc[SUBLANES * g:SUBLANES * (g + 1), :] for g in range(KEY_GROUPS)] for sc in scs]
    ks = [[base + float(SUBLANES * g) for g in range(KEY_GROUPS)] for _ in scs]
    for a, b in _SORT_NETWORK:
        for x in range(n):
            v, k = vs[x], ks[x]
            order = jnp.where(v[a] == v[b], k[b] - k[a], v[a] - v[b])
            swap = order < 0.0
            v[a], v[b] = jnp.where(swap, v[b], v[a]), jnp.where(swap, v[a], v[b])
            k[a], k[b] = jnp.where(swap, k[b], k[a]), jnp.where(swap, k[a], k[b])
    vals = [[] for _ in range(n)]
    keys = [[] for _ in range(n)]
    for r in range(TOPK):
        for x in range(n):
            v, k = vs[x], ks[x]
            m = jnp.max(v[0], axis=0, keepdims=True)
            key = jnp.min(jnp.where(v[0] == m, k[0], float(N_KEYS)), axis=0, keepdims=True)
            hit = k[0] == key
            vals[x].append(m)
            keys[x].append(key)
            for j in range(TOPK - 1 - r):
                v[j] = jnp.where(hit, v[j + 1], v[j])
                k[j] = jnp.where(hit, k[j + 1], k[j])
    return [(jnp.concatenate(vals[x], axis=0), jnp.concatenate(keys[x], axis=0)) for x in range(n)]


_CAND_PAIRS = tuple((i, j) for i in range(TOPK) for j in range(TOPK) if (i + 1) * (j + 1) <= TOPK)
N_CAND = -(-len(_CAND_PAIRS) // SUBLANES) * SUBLANES


def _cand_constants():
    n = len(_CAND_PAIRS)
    ii = jnp.array([p[0] for p in _CAND_PAIRS] + [-1] * (N_CAND - n))
    jj = jnp.array([p[1] for p in _CAND_PAIRS] + [-1] * (N_CAND - n))
    cols = jnp.arange(TOPK)
    sel_i = (ii[:, None] == cols[None, :]).astype(F32)
    sel_j = (jj[:, None] == cols[None, :]).astype(F32)
    pad = jnp.where(jnp.arange(N_CAND) < n, 0.0, -jnp.inf).astype(F32).reshape(N_CAND, 1)
    return sel_i, sel_j, pad


def _pick_rows(sel, a):
    return _dot_f32_by_01(sel, a, False)


def _route_kernel(x_ref, g_ref, wq_ref, sk_ref, seli_ref, selj_ref, pad_ref,
                  hp_ref, idx_ref, par_ref, gate_ref):
    h = _rms(x_ref[...], g_ref[...])
    hb = h.astype(BF16)
    _store_rows_as_sublanes(hp_ref, _pack_words(h))
    q_t = _dot_nt(wq_ref[...], hb)
    sel_i, sel_j, pad = seli_ref[...], selj_ref[...], pad_ref[...]
    eids, gates = [], []
    for h0 in range(0, P_HEADS, ROUTE_HEAD_GROUP):
        scs = []
        for hd in range(h0, h0 + ROUTE_HEAD_GROUP):
            for half in range(2):
                r0 = (hd * 2 + half) * P_HALF
                scs.append(_dot(sk_ref[hd, half], q_t[r0:r0 + P_HALF, :].astype(BF16)))
        tops = _top16_keys(scs)
        cands, cids = [], []
        for g in range(ROUTE_HEAD_GROUP):
            (sv0, si0), (sv1, si1) = tops[2 * g], tops[2 * g + 1]
            cands.append(_pick_rows(sel_i, sv0) + _pick_rows(sel_j, sv1) + pad)
            cids.append(_pick_rows(sel_i, si0) * float(N_KEYS) + _pick_rows(sel_j, si1))
        for fv, eid in _top16_rows(cands, payloads=cids):
            e = jnp.exp(fv - fv[0:1, :])
            gates.append(e / jnp.sum(e, axis=0, keepdims=True))
            eids.append(eid)
    eid_t = jnp.concatenate(eids, axis=0)
    gate_ref[0] = jnp.concatenate(gates, axis=0)
    pair = jnp.floor(eid_t * 0.5)
    par_ref[...] = (eid_t - 2.0 * pair).T
    idx_ref[...] = (pair * float(SUBLANES)).T.astype(jnp.int32)


def _route(x2d, p):
    tr = ROUTE_TILE
    nt = x2d.shape[0]
    nb = nt // tr
    sel_i, sel_j, pad = _cand_constants()
    const2 = lambda i: (0, 0)
    return pl.pallas_call(
        _route_kernel,
        grid=(nb,),
        in_specs=[pl.BlockSpec((tr, D_MODEL), lambda i: (i, 0)),
                  pl.BlockSpec((1, D_MODEL), const2),
                  pl.BlockSpec(p["w_pq_t"].shape, const2),
                  pl.BlockSpec(p["sub_keys"].shape, lambda i: (0, 0, 0, 0)),
                  pl.BlockSpec(sel_i.shape, const2), pl.BlockSpec(sel_j.shape, const2),
                  pl.BlockSpec(pad.shape, const2)],
        out_specs=[pl.BlockSpec((tr * ROW_SUBLANES, LANES), lambda i: (i, 0)),
                   pl.BlockSpec((tr, N_SEL), lambda i: (i, 0)),
                   pl.BlockSpec((tr, N_SEL), lambda i: (i, 0)),
                   pl.BlockSpec((1, N_SEL, tr), lambda i: (i, 0, 0))],
        out_shape=[jax.ShapeDtypeStruct((nt * ROW_SUBLANES, LANES), jnp.uint32),
                   jax.ShapeDtypeStruct((nt, N_SEL), jnp.int32),
                   jax.ShapeDtypeStruct((nt, N_SEL), F32),
                   jax.ShapeDtypeStruct((nb, N_SEL, tr), F32)],
        name="peer_route",
    )(x2d, p["g_ffn"], p["w_pq_t"], p["sub_keys"], sel_i, sel_j, pad)


TILE_ROWS = 2 * SUBLANES
EXPERT_ROWS = SUBLANES
STACK_ROWS = N_SEL * TILE_ROWS
CHUNK_TILES = 16
PEER_UNROLL = 16


def _table_tile(tbl_ref, row8):
    return pltpu.bitcast(tbl_ref[pl.ds(pl.multiple_of(row8, SUBLANES), SUBLANES), :], BF16)


def _peer_constants():
    r = jnp.arange(STACK_ROWS)
    c = jnp.arange(2 * N_SEL)
    rep = ((c[:, None] % N_SEL == r[None, :] // TILE_ROWS)
           & (c[:, None] // N_SEL == (r[None, :] % TILE_ROWS) // EXPERT_ROWS)).astype(BF16)
    rc = jnp.arange(CHUNK_TILES * TILE_ROWS)
    tile_rows = (jnp.arange(CHUNK_TILES)[:, None] == rc[None, :] // TILE_ROWS).astype(F32)
    jj = r % EXPERT_ROWS
    d_rows = (jnp.arange(EXPERT_ROWS)[:, None] == ((jj % 2) * ROW_SUBLANES + jj // 2)[None, :]).astype(F32)
    return rep, tile_rows, d_rows


def _erf_gelu(a):
    return 0.5 * a * (1.0 + lax.erf(a * (2.0 ** -0.5)))


def _uside_kernel(idx_ref, hp_ref, par_ref, gate_ref, rep_ref, trow_ref, tbl_ref, w2_ref, pr_ref, acc_ref):
    tb = par_ref.shape[0]
    par = par_ref[...]
    pr_ref[...] = _dot(jnp.concatenate([1.0 - par, par], axis=-1).astype(BF16), rep_ref[...])
    tile_rows = trow_ref[...]
    lane = lax.broadcasted_iota(jnp.int32, (N_SEL, tb), 1)

    def partials(t0):
        hbs, rows = [], []
        for u in range(PEER_UNROLL):
            h4 = hp_ref[pl.ds(pl.multiple_of((t0 + u) * ROW_SUBLANES, ROW_SUBLANES), ROW_SUBLANES), :]
            hbs.append(pltpu.bitcast(jnp.concatenate([h4, h4], axis=0), BF16))
            rows.append(idx_ref.at[0, t0 + u])
        outs = [[] for _ in range(PEER_UNROLL)]
        for c in range(N_SEL // CHUNK_TILES):
            prods = [[] for _ in range(PEER_UNROLL)]
            for q in range(CHUNK_TILES):
                for u in range(PEER_UNROLL):
                    prods[u].append(_table_tile(tbl_ref, rows[u][c * CHUNK_TILES + q]) * hbs[u])
            for u in range(PEER_UNROLL):
                k0 = c * CHUNK_TILES * TILE_ROWS
                pick = (pr_ref[pl.ds(t0 + u, 1), k0:k0 + CHUNK_TILES * TILE_ROWS] * tile_rows).astype(BF16)
                outs[u].append(_dot(pick, jnp.concatenate(prods[u], axis=0)))
        return tuple(jnp.concatenate(o, axis=0) for o in outs)

    def finish(t0, parts):
        for u in range(PEER_UNROLL):
            s = jnp.sum(parts[u], axis=-1, keepdims=True)
            acc_ref[...] = jnp.where(lane == t0 + u, s, acc_ref[...])

    def body(i, prev):
        parts = partials(i * PEER_UNROLL)
        finish((i - 1) * PEER_UNROLL, prev)
        return parts

    last = lax.fori_loop(1, tb // PEER_UNROLL, body, partials(0))
    finish(tb - PEER_UNROLL, last)
    w = (gate_ref[0] * _erf_gelu(acc_ref[...])).T
    w2_ref[0] = jnp.concatenate([w * (1.0 - par), w * par], axis=-1)


def _uside(idx3, hp4, par, gate_t, table, consts):
    nb, tb, _ = idx3.shape
    rep, tile_rows, _ = consts
    const2 = lambda i: (0, 0)
    per_route = gate_t.shape[2] // tb
    return pl.pallas_call(
        _uside_kernel,
        grid=(nb,),
        in_specs=[pl.BlockSpec((1, tb, N_SEL), lambda i: (i, 0, 0), memory_space=pltpu.SMEM),
                  pl.BlockSpec((tb * ROW_SUBLANES, LANES), lambda i: (i, 0)),
                  pl.BlockSpec((tb, N_SEL), lambda i: (i, 0)),
                  pl.BlockSpec((1, N_SEL, tb), lambda i: (i // per_route, 0, i % per_route)),
                  pl.BlockSpec(rep.shape, const2),
                  pl.BlockSpec(tile_rows.shape, const2),
                  pl.BlockSpec(table.shape, const2, pipeline_mode=pl.Buffered(1))],
        out_specs=pl.BlockSpec((1, tb, 2 * N_SEL), lambda i: (i, 0, 0)),
        out_shape=jax.ShapeDtypeStruct((nb, tb, 2 * N_SEL), F32),
        scratch_shapes=[pltpu.VMEM((tb, STACK_ROWS), F32), pltpu.VMEM((N_SEL, tb), F32)],
        compiler_params=pltpu.CompilerParams(vmem_limit_bytes=VMEM_TABLE_LIMIT),
        name="peer_u",
    )(idx3, hp4, par, gate_t, rep, tile_rows, table)


def _vside_kernel(idx_ref, w2_ref, rep_ref, drow_ref, tbl_ref, o_ref, whi_ref, wlo_ref):
    tb = o_ref.shape[0]
    w2 = w2_ref[0]
    w_hi = w2.astype(BF16)
    w_lo = (w2 - w_hi.astype(F32)).astype(BF16)
    whi_ref[...] = _dot(w_hi, rep_ref[...])
    wlo_ref[...] = _dot(w_lo, rep_ref[...])
    d_rows = drow_ref[...]

    def body(i, carry):
        t0 = i * PEER_UNROLL
        rows = [idx_ref.at[0, t0 + u] for u in range(PEER_UNROLL)]
        accs = [jnp.zeros((2 * EXPERT_ROWS, LANES), F32) for _ in range(PEER_UNROLL)]
        for c in range(N_SEL // CHUNK_TILES):
            k0 = c * CHUNK_TILES * TILE_ROWS
            kr = slice(k0, k0 + CHUNK_TILES * TILE_ROWS)
            tiles = [[] for _ in range(PEER_UNROLL)]
            for q in range(CHUNK_TILES):
                for u in range(PEER_UNROLL):
                    tiles[u].append(_table_tile(tbl_ref, rows[u][c * CHUNK_TILES + q]))
            for u in range(PEER_UNROLL):
                t = t0 + u
                lhs = jnp.concatenate([whi_ref[pl.ds(t, 1), kr] * d_rows[:, kr],
                                       wlo_ref[pl.ds(t, 1), kr] * d_rows[:, kr]], axis=0).astype(BF16)
                accs[u] = accs[u] + _dot(lhs, jnp.concatenate(tiles[u], axis=0))
        ys = [a[0:EXPERT_ROWS, :] + a[EXPERT_ROWS:2 * EXPERT_ROWS, :] for a in accs]
        t0a = pl.multiple_of(t0, PEER_UNROLL)
        for m in range(EXPERT_ROWS):
            o_ref[pl.ds(t0a, PEER_UNROLL), m * LANES:(m + 1) * LANES] = jnp.concatenate(
                [y[m:m + 1, :] for y in ys], axis=0)
        return carry

    lax.fori_loop(0, tb // PEER_UNROLL, body, 0)


def _vside(idx3, w2, table, consts):
    nb, tb, _ = idx3.shape
    rep, _, d_rows = consts
    const2 = lambda i: (0, 0)
    return pl.pallas_call(
        _vside_kernel,
        grid=(nb,),
        in_specs=[pl.BlockSpec((1, tb, N_SEL), lambda i: (i, 0, 0), memory_space=pltpu.SMEM),
                  pl.BlockSpec((1, tb, 2 * N_SEL), lambda i: (i, 0, 0)),
                  pl.BlockSpec(rep.shape, const2),
                  pl.BlockSpec(d_rows.shape, const2),
                  pl.BlockSpec(table.shape, const2, pipeline_mode=pl.Buffered(1))],
        out_specs=pl.BlockSpec((tb, D_MODEL), lambda i: (i, 0)),
        out_shape=jax.ShapeDtypeStruct((nb * tb, D_MODEL), F32),
        scratch_shapes=[pltpu.VMEM((tb, STACK_ROWS), F32), pltpu.VMEM((tb, STACK_ROWS), F32)],
        compiler_params=pltpu.CompilerParams(vmem_limit_bytes=VMEM_TABLE_LIMIT),
        name="peer_v",
    )(idx3, w2, rep, d_rows, table)


def _final_kernel(x_ref, y_ref, g_ref, o_ref):
    o_ref[...] = _rms(x_ref[...] + y_ref[...], g_ref[...])


def _final(x2d, y2d, g):
    ts = NORM_TILE
    tok = pl.BlockSpec((ts, D_MODEL), lambda i: (i, 0))
    return pl.pallas_call(
        _final_kernel,
        grid=(x2d.shape[0] // ts,),
        in_specs=[tok, tok, pl.BlockSpec((1, D_MODEL), lambda i: (0, 0))],
        out_specs=tok,
        out_shape=jax.ShapeDtypeStruct(x2d.shape, F32),
        name="final_norm",
    )(x2d, y2d, g)


def _pack_words(x):
    xb = x.astype(BF16)
    lo = lax.bitcast_convert_type(xb[:, 0:HALF_D].astype(F32), jnp.uint32) >> 16
    hi = lax.bitcast_convert_type(xb[:, HALF_D:D_MODEL].astype(F32), jnp.uint32) & jnp.uint32(HI_MASK)
    return lo | hi


def _store_rows_as_sublanes(o_ref, words):
    t = words.shape[0]
    for s in range(ROW_SUBLANES):
        o_ref[pl.ds(s, t, stride=ROW_SUBLANES), :] = words[:, s * LANES:(s + 1) * LANES]


def _pack_kernel(x_ref, o_ref):
    _store_rows_as_sublanes(o_ref, _pack_words(x_ref[0]))


def _pack_table(tables, layer):
    n = tables.shape[1]
    return pl.pallas_call(
        _pack_kernel,
        grid=(n // PACK_TILE,),
        in_specs=[pl.BlockSpec((1, PACK_TILE, D_MODEL), lambda i: (layer, i, 0))],
        out_specs=pl.BlockSpec((PACK_TILE * ROW_SUBLANES, LANES), lambda i: (i, 0)),
        out_shape=jax.ShapeDtypeStruct((n * ROW_SUBLANES, LANES), jnp.uint32),
        name="pack_table",
    )(tables)


def _layer_params(l, g_mix, w_in, b_if, conv_w, conv_b, g_head, w_out, g_xattn, g_mem,
                  w_xq, w_xk, w_xv, w_xo, g_ffn, w_pq, sub_keys, u_experts, v_experts):
    w_gate = w_in[l][:, MAIN_COLS:]
    row = lambda a: a.reshape(1, -1)
    return {
        "g_mix": row(g_mix[l]),
        "w_main": w_in[l][:, :MAIN_COLS].astype(BF16),
        "w_gate_col": jnp.pad(w_gate, ((0, 0), (0, LANES - N_GATES))).astype(BF16),
        "w_gate_row": w_gate.T.astype(BF16),
        "b_col": jnp.pad(b_if[l], (0, LANES - N_GATES)).reshape(1, LANES),
        "b_row": b_if[l].reshape(N_GATES, 1),
        "conv_w": conv_w[l],
        "conv_b": row(conv_b[l]),
        "g_head": row(g_head[l]),
        "w_out": w_out[l].astype(BF16),
        "g_xattn": row(g_xattn[l]),
        "g_mem": row(g_mem[l]),
        "w_xq": w_xq[l].astype(BF16),
        "w_xk": w_xk[l].astype(BF16),
        "w_xv": w_xv[l].astype(BF16),
        "w_xo": w_xo[l].astype(BF16),
        "g_ffn": row(g_ffn[l]),
        "w_pq_t": w_pq[l].T.astype(BF16),
        "sub_keys": sub_keys[l].astype(BF16),
        "u_table": _pack_table(u_experts, l),
        "v_table": _pack_table(v_experts, l),
    }


def kernel(x, mem, g_mix, w_in, b_if, conv_w, conv_b, g_head, w_out, g_xattn, g_mem, w_xq, w_xk, w_xv, w_xo, g_ffn, w_pq, sub_keys, u_experts, v_experts, g_final):
    batch, seq, d = x.shape
    depth = w_in.shape[0]
    nt = batch * seq
    assert d == D_MODEL and seq % XATT_TILE == 0 and seq % MIX_TILE == 0 and nt % ROUTE_TILE == 0
    x2d = x.reshape(nt, d)
    resid = None
    consts = _peer_constants()
    for l in range(depth):
        p = _layer_params(l, g_mix, w_in, b_if, conv_w, conv_b, g_head, w_out, g_xattn, g_mem,
                          w_xq, w_xk, w_xv, w_xo, g_ffn, w_pq, sub_keys, u_experts, v_experts)
        x2d = _mixer(x2d, resid, batch, seq, p)
        k, v = _kv(mem, p)
        x2d = _xattn(x2d, k, v, seq, p)
        hp4, idx, par, gate_t = _route(x2d, p)
        idx3 = idx.reshape(nt // PEER_TILE, PEER_TILE, N_SEL)
        w2 = _uside(idx3, hp4, par, gate_t, p["u_table"], consts)
        resid = _vside(idx3, w2, p["v_table"], consts)
    return _final(x2d, resid, g_final.reshape(1, d)).reshape(batch, seq, d)
```

```python
import functools

import jax
import jax.numpy as jnp
from jax import lax
from jax.experimental import pallas as pl
from jax.experimental.pallas import tpu as pltpu

F32 = jnp.float32
BF16 = jnp.bfloat16

EPS = 1e-6
D_MODEL = 1024
CONV_DIM = 512
CONV_K = 3
LSTM_HEADS = 4
LSTM_DH = 128
LSTM_DIM = LSTM_HEADS * LSTM_DH
CHUNK = 128
N_GATES = 2 * LSTM_HEADS
MAIN_COLS = 3 * CONV_DIM + 4 * LSTM_DIM
OFF_Q = 3 * CONV_DIM
N_MEM = 256
X_HEADS = 4
X_DH = D_MODEL // X_HEADS
P_HEADS = 8
N_KEYS = 128
P_HALF = 128
TOPK = 16
N_SEL = P_HEADS * TOPK
HALF_D = D_MODEL // 2
ROW_SUBLANES = HALF_D // 128

LANES = 128
SUBLANES = 8
VMEM_TABLE_LIMIT = 56 * 1024 * 1024

MIX_TILE = 256
XATT_TILE = 512
ROUTE_TILE = 512
ROUTE_HEAD_GROUP = 4
PEER_TILE = 128
assert ROUTE_TILE % PEER_TILE == 0
NORM_TILE = 512
PACK_TILE = 256

HI_MASK = 0xFFFF0000


def _rms(x, g):
    return x * lax.rsqrt(jnp.mean(x * x, axis=-1, keepdims=True) + EPS) * g


def _dot(a, b):
    return jnp.dot(a, b, preferred_element_type=F32)


def _dot_nt(a, b):
    return lax.dot_general(a, b, (((1,), (1,)), ((), ())), preferred_element_type=F32)


def _dot_tn(a, b):
    return lax.dot_general(a, b, (((0,), (0,)), ((), ())), preferred_element_type=F32)


def _bf16_parts(a):
    hi = a.astype(BF16)
    r = a - hi.astype(F32)
    mid = r.astype(BF16)
    lo = (r - mid.astype(F32)).astype(BF16)
    return hi, mid, lo


def _dot_f32_by_01(a, b, b_is_01):
    if b_is_01:
        return sum(_dot(p, b.astype(BF16)) for p in _bf16_parts(a))
    return sum(_dot(a.astype(BF16), p) for p in _bf16_parts(b))


def _log_sigmoid(x):
    return jnp.minimum(x, 0.0) - jnp.log(1.0 + jnp.exp(-jnp.abs(x)))


def _mixer_kernel(*refs, has_resid):
    if has_resid:
        x_ref, y_ref = refs[0], refs[1]
        refs = refs[2:]
    else:
        x_ref = refs[0]
        refs = refs[1:]
    (g_ref, w_ref, wgc_ref, wgr_ref, bcol_ref, brow_ref, cw_ref, cb_ref, gh_ref, wo_ref,
     o_ref, ct_ref, m_ref, ubuf_ref) = refs
    ts = x_ref.shape[0]

    @pl.when(pl.program_id(1) == 0)
    def _():
        ct_ref[...] = jnp.zeros_like(ct_ref)
        m_ref[...] = jnp.zeros_like(m_ref)
        ubuf_ref[0:SUBLANES, :] = jnp.zeros((SUBLANES, CONV_DIM), F32)

    x = x_ref[...]
    if has_resid:
        x = x + y_ref[...]
    h = _rms(x, g_ref[...]).astype(BF16)

    xin = _dot(h, w_ref[:, 0:CONV_DIM])
    cg = _dot(h, w_ref[:, 2 * CONV_DIM:3 * CONV_DIM])
    u = cg * xin
    ubuf_ref[SUBLANES:SUBLANES + ts, :] = u
    u1 = ubuf_ref[SUBLANES - 1:SUBLANES - 1 + ts, :]
    u2 = ubuf_ref[SUBLANES - 2:SUBLANES - 2 + ts, :]
    conv = cw_ref[0:1, :] * u2 + cw_ref[1:2, :] * u1 + cw_ref[2:3, :] * u
    ubuf_ref[0:SUBLANES, :] = u[ts - SUBLANES:ts, :]
    bg = _dot(h, w_ref[:, CONV_DIM:2 * CONV_DIM])
    y_conv = bg * (conv + cb_ref[...])

    gcol = _dot(h, wgc_ref[...]) + bcol_ref[...]
    grow = _dot_nt(wgr_ref[...], h) + brow_ref[...]
    lf_col = _log_sigmoid(gcol)
    lf_row = _log_sigmoid(grow)
    r_i = lax.broadcasted_iota(jnp.int32, (CHUNK, CHUNK), 0)
    c_i = lax.broadcasted_iota(jnp.int32, (CHUNK, CHUNK), 1)
    causal = c_i <= r_i
    tri_lo = causal.astype(F32)
    tri_up = (r_i <= c_i).astype(F32)
    ones_col = (lax.broadcasted_iota(jnp.int32, (CHUNK, LSTM_DH), 1) == 0).astype(F32)

    zq = _dot(h, w_ref[:, OFF_Q:OFF_Q + LSTM_DIM])
    zk = _dot(h, w_ref[:, OFF_Q + LSTM_DIM:OFF_Q + 2 * LSTM_DIM])
    zv = _dot(h, w_ref[:, OFF_Q + 2 * LSTM_DIM:OFF_Q + 3 * LSTM_DIM])
    zo = _dot(h, w_ref[:, OFF_Q + 3 * LSTM_DIM:OFF_Q + 4 * LSTM_DIM])

    y_chunks = []
    for c in range(ts // CHUNK):
        r0 = c * CHUNK
        gcol_c = gcol[r0:r0 + CHUNK, :]
        grow_c = grow[:, r0:r0 + CHUNK]
        bcol_c = _dot_f32_by_01(tri_lo, lf_col[r0:r0 + CHUNK, :], False)
        brow_c = _dot_f32_by_01(lf_row[:, r0:r0 + CHUNK], tri_up, True)
        y_heads = []
        for hd in range(LSTM_HEADS):
            hs = slice(hd * LSTM_DH, (hd + 1) * LSTM_DH)
            q = zq[r0:r0 + CHUNK, hs].astype(BF16)
            k = (zk[r0:r0 + CHUNK, hs] * (LSTM_DH ** -0.5)).astype(BF16)
            v = zv[r0:r0 + CHUNK, hs]
            og = zo[r0:r0 + CHUNK, hs]
            i_col = gcol_c[:, hd:hd + 1]
            i_row = grow_c[hd:hd + 1, :]
            b_col = bcol_c[:, LSTM_HEADS + hd:LSTM_HEADS + hd + 1]
            b_row = brow_c[LSTM_HEADS + hd:LSTM_HEADS + hd + 1, :]
            m_prev = m_ref[hd][0:1, 0:1]
            ct = ct_ref[hd]

            dm = jnp.where(causal, b_col - b_row + i_row, -jnp.inf)
            inter = b_col + m_prev
            m_t = jnp.maximum(inter, jnp.max(dm, axis=-1, keepdims=True))
            dw = jnp.exp(dm - m_t)
            iw = jnp.exp(inter - m_t)
            s = _dot_nt(q, k) * dw
            v_aug = jnp.concatenate([v, ones_col], axis=-1)
            num_aug = _dot(s.astype(BF16), v_aug.astype(BF16)) + iw * _dot(q, ct.astype(BF16))
            num = num_aug[:, 0:LSTM_DH]
            den = num_aug[:, LSTM_DH:LSTM_DH + 1]
            ht = num / jnp.maximum(jnp.abs(den), jnp.exp(-m_t))

            b_last = b_col[CHUNK - 1:CHUNK, :]
            gsum = b_last - b_col + i_col
            m_new = jnp.maximum(b_last + m_prev, jnp.max(gsum, axis=0, keepdims=True))
            wdec = jnp.exp(b_last + m_prev - m_new)
            ws = jnp.exp(gsum - m_new)
            ct_ref[hd] = wdec * ct + _dot_tn(k, (v_aug * ws).astype(BF16))
            m_ref[hd] = jnp.broadcast_to(m_new, (SUBLANES, LANES))

            hl = jax.nn.sigmoid(og) * ht
            hl = hl * lax.rsqrt(jnp.mean(hl * hl, axis=-1, keepdims=True) + EPS)
            y_heads.append(hl * gh_ref[:, hd * LSTM_DH:(hd + 1) * LSTM_DH])
        y_chunks.append(jnp.concatenate(y_heads, axis=-1))
    y_lstm = jnp.concatenate(y_chunks, axis=0) if len(y_chunks) > 1 else y_chunks[0]
    y = jnp.concatenate([y_conv, y_lstm], axis=-1).astype(BF16)
    o_ref[...] = x + _dot(y, wo_ref[...])


def _mixer(x2d, resid, batch, seq, p):
    ts = MIX_TILE
    n_s = seq // ts
    tok = lambda b, i: (b * n_s + i, 0)
    const2 = lambda b, i: (0, 0)
    in_specs = [pl.BlockSpec((ts, D_MODEL), tok)]
    args = [x2d]
    if resid is not None:
        in_specs.append(pl.BlockSpec((ts, D_MODEL), tok))
        args.append(resid)
    consts = [p["g_mix"], p["w_main"], p["w_gate_col"], p["w_gate_row"], p["b_col"], p["b_row"],
              p["conv_w"], p["conv_b"], p["g_head"], p["w_out"]]
    for a in consts:
        in_specs.append(pl.BlockSpec(a.shape, const2))
        args.append(a)
    return pl.pallas_call(
        functools.partial(_mixer_kernel, has_resid=resid is not None),
        grid=(batch, n_s),
        in_specs=in_specs,
        out_specs=pl.BlockSpec((ts, D_MODEL), tok),
        out_shape=jax.ShapeDtypeStruct(x2d.shape, F32),
        scratch_shapes=[
            pltpu.VMEM((LSTM_HEADS, LSTM_DH, 2 * LSTM_DH), F32),
            pltpu.VMEM((LSTM_HEADS, SUBLANES, LANES), F32),
            pltpu.VMEM((SUBLANES + ts, CONV_DIM), F32),
        ],
        compiler_params=pltpu.CompilerParams(dimension_semantics=("arbitrary", "arbitrary")),
        name="mixer",
    )(*args)


def _kv_kernel(mem_ref, g_ref, wk_ref, wv_ref, k_ref, v_ref):
    memn = _rms(mem_ref[0], g_ref[...]).astype(BF16)
    k_ref[0] = _dot(memn, wk_ref[...]).astype(BF16)
    v_ref[0] = _dot(memn, wv_ref[...]).astype(BF16)


def _kv(mem, p):
    batch = mem.shape[0]
    blk = pl.BlockSpec((1, N_MEM, D_MODEL), lambda b: (b, 0, 0))
    wspec = pl.BlockSpec((D_MODEL, D_MODEL), lambda b: (0, 0))
    return pl.pallas_call(
        _kv_kernel,
        grid=(batch,),
        in_specs=[blk, pl.BlockSpec((1, D_MODEL), lambda b: (0, 0)), wspec, wspec],
        out_specs=[blk, blk],
        out_shape=[jax.ShapeDtypeStruct(mem.shape, BF16)] * 2,
        name="xattn_kv",
    )(mem, p["g_mem"], p["w_xk"], p["w_xv"])


def _xattn_kernel(x_ref, g_ref, wq_ref, k_ref, v_ref, wo_ref, o_ref):
    x = x_ref[...]
    h = _rms(x, g_ref[...]).astype(BF16)
    q = _dot(h, wq_ref[...])
    outs = []
    for hd in range(X_HEADS):
        sl = slice(hd * X_DH, (hd + 1) * X_DH)
        s = _dot_nt(q[:, sl].astype(BF16), k_ref[0, :, sl]) * (X_DH ** -0.5)
        e = jnp.exp(s - jnp.max(s, axis=-1, keepdims=True))
        pr = e / jnp.sum(e, axis=-1, keepdims=True)
        outs.append(_dot(pr.astype(BF16), v_ref[0, :, sl]))
    o = jnp.concatenate(outs, axis=-1).astype(BF16)
    o_ref[...] = x + _dot(o, wo_ref[...])


def _xattn(x2d, k, v, seq, p):
    ts = XATT_TILE
    per_b = seq // ts
    tok = lambda i: (i, 0)
    wspec = pl.BlockSpec((D_MODEL, D_MODEL), lambda i: (0, 0))
    kvspec = pl.BlockSpec((1, N_MEM, D_MODEL), lambda i: (i // per_b, 0, 0))
    return pl.pallas_call(
        _xattn_kernel,
        grid=(x2d.shape[0] // ts,),
        in_specs=[pl.BlockSpec((ts, D_MODEL), tok), pl.BlockSpec((1, D_MODEL), lambda i: (0, 0)),
                  wspec, kvspec, kvspec, wspec],
        out_specs=pl.BlockSpec((ts, D_MODEL), tok),
        out_shape=jax.ShapeDtypeStruct(x2d.shape, F32),
        name="xattn",
    )(x2d, p["g_xattn"], p["w_xq"], k, v, p["w_xo"])


def _top16_rows(scs, payloads=None):
    scs = list(scs)
    n = len(scs)
    rows = scs[0].shape[0]
    iota = lax.broadcasted_iota(jnp.int32, scs[0].shape, 0).astype(F32)
    vals = [[] for _ in range(n)]
    picks = [[] for _ in range(n)]
    for _ in range(TOPK):
        for a in range(n):
            sc = scs[a]
            m = jnp.max(sc, axis=0, keepdims=True)
            idx = jnp.min(jnp.where(sc == m, iota, float(rows)), axis=0, keepdims=True)
            hit = iota == idx
            vals[a].append(m)
            if payloads is None:
                picks[a].append(idx)
            else:
                picks[a].append(jnp.max(jnp.where(hit, payloads[a], -1.0), axis=0, keepdims=True))
            scs[a] = jnp.where(hit, -jnp.inf, sc)
    return [(jnp.concatenate(vals[a], axis=0), jnp.concatenate(picks[a], axis=0)) for a in range(n)]


def _oddeven_merge_sort_pairs(n):
    pairs = []
    p = 1
    while p < n:
        k = p
        while k >= 1:
            for j in range(k % p, n - k, 2 * k):
                for i in range(min(k, n - j - k)):
                    if (i + j) // (2 * p) == (i + j + k) // (2 * p):
                        pairs.append((i + j, i + j + k))
            k //= 2
        p *= 2
    return tuple(pairs)


KEY_GROUPS = N_KEYS // SUBLANES
_SORT_NETWORK = _oddeven_merge_sort_pairs(KEY_GROUPS)


def _top16_keys(scs):
    n = len(scs)
    t = scs[0].shape[1]
    base = lax.broadcasted_iota(jnp.int32, (SUBLANES, t), 0).astype(F32)
    vs = [[sc[SUBLANES * g:SUBLANES * (g + 1), :] for g in range(KEY_GROUPS)] for sc in scs]
    ks = [[base + float(SUBLANES * g) for g in range(KEY_GROUPS)] for _ in scs]
    for a, b in _SORT_NETWORK:
        for x in range(n):
            v, k = vs[x], ks[x]
            order = jnp.where(v[a] == v[b], k[b] - k[a], v[a] - v[b])
            swap = order < 0.0
            v[a], v[b] = jnp.where(swap, v[b], v[a]), jnp.where(swap, v[a], v[b])
            k[a], k[b] = jnp.where(swap, k[b], k[a]), jnp.where(swap, k[a], k[b])
    vals = [[] for _ in range(n)]
    keys = [[] for _ in range(n)]
    for r in range(TOPK):
        for x in range(n):
            v, k = vs[x], ks[x]
            m = jnp.max(v[0], axis=0, keepdims=True)
            key = jnp.min(jnp.where(v[0] == m, k[0], float(N_KEYS)), axis=0, keepdims=True)
            hit = k[0] == key
            vals[x].append(m)
            keys[x].append(key)
            for j in range(TOPK - 1 - r):
                v[j] = jnp.where(hit, v[j + 1], v[j])
                k[j] = jnp.where(hit, k[j + 1], k[j])
    return [(jnp.concatenate(vals[x], axis=0), jnp.concatenate(keys[x], axis=0)) for x in range(n)]


CAND_SLABS = 8
_CAND_NETWORK = _oddeven_merge_sort_pairs(CAND_SLABS)


def _top16_cands(cands, payloads):
    n = len(cands)
    t = cands[0].shape[1]
    real = cands[0].shape[0] // SUBLANES
    base = lax.broadcasted_iota(jnp.int32, (SUBLANES, t), 0).astype(F32)
    neg = jnp.full((SUBLANES, t), -jnp.inf, F32)
    vs = [[c[SUBLANES * g:SUBLANES * (g + 1), :] if g < real else neg for g in range(CAND_SLABS)] for c in cands]
    ks = [[base + float(SUBLANES * g) for g in range(CAND_SLABS)] for _ in cands]
    ps = [[p[SUBLANES * g:SUBLANES * (g + 1), :] if g < real else neg for g in range(CAND_SLABS)] for p in payloads]
    for a, b in _CAND_NETWORK:
        for x in range(n):
            v, k, p = vs[x], ks[x], ps[x]
            order = jnp.where(v[a] == v[b], k[b] - k[a], v[a] - v[b])
            swap = order < 0.0
            v[a], v[b] = jnp.where(swap, v[b], v[a]), jnp.where(swap, v[a], v[b])
            k[a], k[b] = jnp.where(swap, k[b], k[a]), jnp.where(swap, k[a], k[b])
            p[a], p[b] = jnp.where(swap, p[b], p[a]), jnp.where(swap, p[a], p[b])
    vals = [[] for _ in range(n)]
    pays = [[] for _ in range(n)]
    for _ in range(TOPK):
        for x in range(n):
            v, k, p = vs[x], ks[x], ps[x]
            m = jnp.max(v[0], axis=0, keepdims=True)
            key = jnp.min(jnp.where(v[0] == m, k[0], float(CAND_SLABS * SUBLANES)), axis=0, keepdims=True)
            hit = k[0] == key
            vals[x].append(m)
            pays[x].append(jnp.max(jnp.where(hit, p[0], -1.0), axis=0, keepdims=True))
            for j in range(CAND_SLABS - 1):
                v[j] = jnp.where(hit, v[j + 1], v[j])
                k[j] = jnp.where(hit, k[j + 1], k[j])
                p[j] = jnp.where(hit, p[j + 1], p[j])
            v[CAND_SLABS - 1] = jnp.where(hit, -jnp.inf, v[CAND_SLABS - 1])
    return [(jnp.concatenate(vals[x], axis=0), jnp.concatenate(pays[x], axis=0)) for x in range(n)]


_CAND_PAIRS = tuple((i, j) for i in range(TOPK) for j in range(TOPK) if (i + 1) * (j + 1) <= TOPK)
N_CAND = -(-len(_CAND_PAIRS) // SUBLANES) * SUBLANES


def _cand_constants():
    n = len(_CAND_PAIRS)
    ii = jnp.array([p[0] for p in _CAND_PAIRS] + [-1] * (N_CAND - n))
    jj = jnp.array([p[1] for p in _CAND_PAIRS] + [-1] * (N_CAND - n))
    cols = jnp.arange(TOPK)
    sel_i = (ii[:, None] == cols[None, :]).astype(F32)
    sel_j = (jj[:, None] == cols[None, :]).astype(F32)
    pad = jnp.where(jnp.arange(N_CAND) < n, 0.0, -jnp.inf).astype(F32).reshape(N_CAND, 1)
    return sel_i, sel_j, pad


def _pick_rows(sel, a):
    return _dot_f32_by_01(sel, a, False)


def _route_kernel(x_ref, g_ref, wq_ref, sk_ref, seli_ref, selj_ref, pad_ref,
                  hp_ref, idx_ref, par_ref, gate_ref):
    h = _rms(x_ref[...], g_ref[...])
    hb = h.astype(BF16)
    _store_rows_as_sublanes(hp_ref, _pack_words(h))
    q_t = _dot_nt(wq_ref[...], hb)
    sel_i, sel_j, pad = seli_ref[...], selj_ref[...], pad_ref[...]
    eids, gates = [], []
    for h0 in range(0, P_HEADS, ROUTE_HEAD_GROUP):
        scs = []
        for hd in range(h0, h0 + ROUTE_HEAD_GROUP):
            for half in range(2):
                r0 = (hd * 2 + half) * P_HALF
                scs.append(_dot(sk_ref[hd, half], q_t[r0:r0 + P_HALF, :].astype(BF16)))
        tops = _top16_keys(scs)
        cands, cids = [], []
        for g in range(ROUTE_HEAD_GROUP):
            (sv0, si0), (sv1, si1) = tops[2 * g], tops[2 * g + 1]
            cands.append(_pick_rows(sel_i, sv0) + _pick_rows(sel_j, sv1) + pad)
            cids.append(_pick_rows(sel_i, si0) * float(N_KEYS) + _pick_rows(sel_j, si1))
        for fv, eid in _top16_cands(cands, cids):
            e = jnp.exp(fv - fv[0:1, :])
            gates.append(e / jnp.sum(e, axis=0, keepdims=True))
            eids.append(eid)
    eid_t = jnp.concatenate(eids, axis=0)
    gate_ref[0] = jnp.concatenate(gates, axis=0)
    pair = jnp.floor(eid_t * 0.5)
    par_ref[...] = (eid_t - 2.0 * pair).T
    idx_ref[...] = (pair * float(SUBLANES)).T.astype(jnp.int32)


def _route(x2d, p):
    tr = ROUTE_TILE
    nt = x2d.shape[0]
    nb = nt // tr
    sel_i, sel_j, pad = _cand_constants()
    const2 = lambda i: (0, 0)
    return pl.pallas_call(
        _route_kernel,
        grid=(nb,),
        in_specs=[pl.BlockSpec((tr, D_MODEL), lambda i: (i, 0)),
                  pl.BlockSpec((1, D_MODEL), const2),
                  pl.BlockSpec(p["w_pq_t"].shape, const2),
                  pl.BlockSpec(p["sub_keys"].shape, lambda i: (0, 0, 0, 0)),
                  pl.BlockSpec(sel_i.shape, const2), pl.BlockSpec(sel_j.shape, const2),
                  pl.BlockSpec(pad.shape, const2)],
        out_specs=[pl.BlockSpec((tr * ROW_SUBLANES, LANES), lambda i: (i, 0)),
                   pl.BlockSpec((tr, N_SEL), lambda i: (i, 0)),
                   pl.BlockSpec((tr, N_SEL), lambda i: (i, 0)),
                   pl.BlockSpec((1, N_SEL, tr), lambda i: (i, 0, 0))],
        out_shape=[jax.ShapeDtypeStruct((nt * ROW_SUBLANES, LANES), jnp.uint32),
                   jax.ShapeDtypeStruct((nt, N_SEL), jnp.int32),
                   jax.ShapeDtypeStruct((nt, N_SEL), F32),
                   jax.ShapeDtypeStruct((nb, N_SEL, tr), F32)],
        name="peer_route",
    )(x2d, p["g_ffn"], p["w_pq_t"], p["sub_keys"], sel_i, sel_j, pad)


TILE_ROWS = 2 * SUBLANES
EXPERT_ROWS = SUBLANES
STACK_ROWS = N_SEL * TILE_ROWS
CHUNK_TILES = 16
PEER_UNROLL = 16


def _table_tile(tbl_ref, row8):
    return pltpu.bitcast(tbl_ref[pl.ds(pl.multiple_of(row8, SUBLANES), SUBLANES), :], BF16)


def _peer_constants():
    r = jnp.arange(STACK_ROWS)
    c = jnp.arange(2 * N_SEL)
    rep = ((c[:, None] % N_SEL == r[None, :] // TILE_ROWS)
           & (c[:, None] // N_SEL == (r[None, :] % TILE_ROWS) // EXPERT_ROWS)).astype(BF16)
    rc = jnp.arange(CHUNK_TILES * TILE_ROWS)
    tile_rows = (jnp.arange(CHUNK_TILES)[:, None] == rc[None, :] // TILE_ROWS).astype(F32)
    jj = r % EXPERT_ROWS
    d_rows = (jnp.arange(EXPERT_ROWS)[:, None] == ((jj % 2) * ROW_SUBLANES + jj // 2)[None, :]).astype(F32)
    return rep, tile_rows, d_rows


def _erf_gelu(a):
    return 0.5 * a * (1.0 + lax.erf(a * (2.0 ** -0.5)))


def _uside_kernel(idx_ref, hp_ref, par_ref, gate_ref, rep_ref, trow_ref, tbl_ref, w2_ref, pr_ref, acc_ref):
    tb = par_ref.shape[0]
    par = par_ref[...]
    pr_ref[...] = _dot(jnp.concatenate([1.0 - par, par], axis=-1).astype(BF16), rep_ref[...])
    tile_rows = trow_ref[...]
    lane = lax.broadcasted_iota(jnp.int32, (N_SEL, tb), 1)

    def partials(t0):
        hbs, rows = [], []
        for u in range(PEER_UNROLL):
            h4 = hp_ref[pl.ds(pl.multiple_of((t0 + u) * ROW_SUBLANES, ROW_SUBLANES), ROW_SUBLANES), :]
            hbs.append(pltpu.bitcast(jnp.concatenate([h4, h4], axis=0), BF16))
            rows.append(idx_ref.at[0, t0 + u])
        outs = [[] for _ in range(PEER_UNROLL)]
        for c in range(N_SEL // CHUNK_TILES):
            prods = [[] for _ in range(PEER_UNROLL)]
            for q in range(CHUNK_TILES):
                for u in range(PEER_UNROLL):
                    prods[u].append(_table_tile(tbl_ref, rows[u][c * CHUNK_TILES + q]) * hbs[u])
            for u in range(PEER_UNROLL):
                k0 = c * CHUNK_TILES * TILE_ROWS
                pick = (pr_ref[pl.ds(t0 + u, 1), k0:k0 + CHUNK_TILES * TILE_ROWS] * tile_rows).astype(BF16)
                outs[u].append(_dot(pick, jnp.concatenate(prods[u], axis=0)))
        return tuple(jnp.concatenate(o, axis=0) for o in outs)

    def finish(t0, parts):
        for u in range(PEER_UNROLL):
            s = jnp.sum(parts[u], axis=-1, keepdims=True)
            acc_ref[...] = jnp.where(lane == t0 + u, s, acc_ref[...])

    def body(i, prev):
        parts = partials(i * PEER_UNROLL)
        finish((i - 1) * PEER_UNROLL, prev)
        return parts

    last = lax.fori_loop(1, tb // PEER_UNROLL, body, partials(0))
    finish(tb - PEER_UNROLL, last)
    w = (gate_ref[0] * _erf_gelu(acc_ref[...])).T
    w2_ref[0] = jnp.concatenate([w * (1.0 - par), w * par], axis=-1)


def _uside(idx3, hp4, par, gate_t, table, consts):
    nb, tb, _ = idx3.shape
    rep, tile_rows, _ = consts
    const2 = lambda i: (0, 0)
    per_route = gate_t.shape[2] // tb
    return pl.pallas_call(
        _uside_kernel,
        grid=(nb,),
        in_specs=[pl.BlockSpec((1, tb, N_SEL), lambda i: (i, 0, 0), memory_space=pltpu.SMEM),
                  pl.BlockSpec((tb * ROW_SUBLANES, LANES), lambda i: (i, 0)),
                  pl.BlockSpec((tb, N_SEL), lambda i: (i, 0)),
                  pl.BlockSpec((1, N_SEL, tb), lambda i: (i // per_route, 0, i % per_route)),
                  pl.BlockSpec(rep.shape, const2),
                  pl.BlockSpec(tile_rows.shape, const2),
                  pl.BlockSpec(table.shape, const2, pipeline_mode=pl.Buffered(1))],
        out_specs=pl.BlockSpec((1, tb, 2 * N_SEL), lambda i: (i, 0, 0)),
        out_shape=jax.ShapeDtypeStruct((nb, tb, 2 * N_SEL), F32),
        scratch_shapes=[pltpu.VMEM((tb, STACK_ROWS), F32), pltpu.VMEM((N_SEL, tb), F32)],
        compiler_params=pltpu.CompilerParams(vmem_limit_bytes=VMEM_TABLE_LIMIT),
        name="peer_u",
    )(idx3, hp4, par, gate_t, rep, tile_rows, table)


def _vside_kernel(idx_ref, w2_ref, rep_ref, drow_ref, tbl_ref, o_ref, whi_ref, wlo_ref):
    tb = o_ref.shape[0]
    w2 = w2_ref[0]
    w_hi = w2.astype(BF16)
    w_lo = (w2 - w_hi.astype(F32)).astype(BF16)
    whi_ref[...] = _dot(w_hi, rep_ref[...])
    wlo_ref[...] = _dot(w_lo, rep_ref[...])
    d_rows = drow_ref[...]

    def body(i, carry):
        t0 = i * PEER_UNROLL
        rows = [idx_ref.at[0, t0 + u] for u in range(PEER_UNROLL)]
        accs = [jnp.zeros((2 * EXPERT_ROWS, LANES), F32) for _ in range(PEER_UNROLL)]
        for c in range(N_SEL // CHUNK_TILES):
            k0 = c * CHUNK_TILES * TILE_ROWS
            kr = slice(k0, k0 + CHUNK_TILES * TILE_ROWS)
            tiles = [[] for _ in range(PEER_UNROLL)]
            for q in range(CHUNK_TILES):
                for u in range(PEER_UNROLL):
                    tiles[u].append(_table_tile(tbl_ref, rows[u][c * CHUNK_TILES + q]))
            for u in range(PEER_UNROLL):
                t = t0 + u
                lhs = jnp.concatenate([whi_ref[pl.ds(t, 1), kr] * d_rows[:, kr],
                                       wlo_ref[pl.ds(t, 1), kr] * d_rows[:, kr]], axis=0).astype(BF16)
                accs[u] = accs[u] + _dot(lhs, jnp.concatenate(tiles[u], axis=0))
        ys = [a[0:EXPERT_ROWS, :] + a[EXPERT_ROWS:2 * EXPERT_ROWS, :] for a in accs]
        t0a = pl.multiple_of(t0, PEER_UNROLL)
        for m in range(EXPERT_ROWS):
            o_ref[pl.ds(t0a, PEER_UNROLL), m * LANES:(m + 1) * LANES] = jnp.concatenate(
                [y[m:m + 1, :] for y in ys], axis=0)
        return carry

    lax.fori_loop(0, tb // PEER_UNROLL, body, 0)


def _vside(idx3, w2, table, consts):
    nb, tb, _ = idx3.shape
    rep, _, d_rows = consts
    const2 = lambda i: (0, 0)
    return pl.pallas_call(
        _vside_kernel,
        grid=(nb,),
        in_specs=[pl.BlockSpec((1, tb, N_SEL), lambda i: (i, 0, 0), memory_space=pltpu.SMEM),
                  pl.BlockSpec((1, tb, 2 * N_SEL), lambda i: (i, 0, 0)),
                  pl.BlockSpec(rep.shape, const2),
                  pl.BlockSpec(d_rows.shape, const2),
                  pl.BlockSpec(table.shape, const2, pipeline_mode=pl.Buffered(1))],
        out_specs=pl.BlockSpec((tb, D_MODEL), lambda i: (i, 0)),
        out_shape=jax.ShapeDtypeStruct((nb * tb, D_MODEL), F32),
        scratch_shapes=[pltpu.VMEM((tb, STACK_ROWS), F32), pltpu.VMEM((tb, STACK_ROWS), F32)],
        compiler_params=pltpu.CompilerParams(vmem_limit_bytes=VMEM_TABLE_LIMIT),
        name="peer_v",
    )(idx3, w2, rep, d_rows, table)


def _final_kernel(x_ref, y_ref, g_ref, o_ref):
    o_ref[...] = _rms(x_ref[...] + y_ref[...], g_ref[...])


def _final(x2d, y2d, g):
    ts = NORM_TILE
    tok = pl.BlockSpec((ts, D_MODEL), lambda i: (i, 0))
    return pl.pallas_call(
        _final_kernel,
        grid=(x2d.shape[0] // ts,),
        in_specs=[tok, tok, pl.BlockSpec((1, D_MODEL), lambda i: (0, 0))],
        out_specs=tok,
        out_shape=jax.ShapeDtypeStruct(x2d.shape, F32),
        name="final_norm",
    )(x2d, y2d, g)


def _pack_words(x):
    xb = x.astype(BF16)
    lo = lax.bitcast_convert_type(xb[:, 0:HALF_D].astype(F32), jnp.uint32) >> 16
    hi = lax.bitcast_convert_type(xb[:, HALF_D:D_MODEL].astype(F32), jnp.uint32) & jnp.uint32(HI_MASK)
    return lo | hi


def _store_rows_as_sublanes(o_ref, words):
    t = words.shape[0]
    for s in range(ROW_SUBLANES):
        o_ref[pl.ds(s, t, stride=ROW_SUBLANES), :] = words[:, s * LANES:(s + 1) * LANES]


def _pack_kernel(x_ref, o_ref):
    _store_rows_as_sublanes(o_ref, _pack_words(x_ref[0]))


def _pack_table(tables, layer):
    n = tables.shape[1]
    return pl.pallas_call(
        _pack_kernel,
        grid=(n // PACK_TILE,),
        in_specs=[pl.BlockSpec((1, PACK_TILE, D_MODEL), lambda i: (layer, i, 0))],
        out_specs=pl.BlockSpec((PACK_TILE * ROW_SUBLANES, LANES), lambda i: (i, 0)),
        out_shape=jax.ShapeDtypeStruct((n * ROW_SUBLANES, LANES), jnp.uint32),
        name="pack_table",
    )(tables)


def _layer_params(l, g_mix, w_in, b_if, conv_w, conv_b, g_head, w_out, g_xattn, g_mem,
                  w_xq, w_xk, w_xv, w_xo, g_ffn, w_pq, sub_keys, u_experts, v_experts):
    w_gate = w_in[l][:, MAIN_COLS:]
    row = lambda a: a.reshape(1, -1)
    return {
        "g_mix": row(g_mix[l]),
        "w_main": w_in[l][:, :MAIN_COLS].astype(BF16),
        "w_gate_col": jnp.pad(w_gate, ((0, 0), (0, LANES - N_GATES))).astype(BF16),
        "w_gate_row": w_gate.T.astype(BF16),
        "b_col": jnp.pad(b_if[l], (0, LANES - N_GATES)).reshape(1, LANES),
        "b_row": b_if[l].reshape(N_GATES, 1),
        "conv_w": conv_w[l],
        "conv_b": row(conv_b[l]),
        "g_head": row(g_head[l]),
        "w_out": w_out[l].astype(BF16),
        "g_xattn": row(g_xattn[l]),
        "g_mem": row(g_mem[l]),
        "w_xq": w_xq[l].astype(BF16),
        "w_xk": w_xk[l].astype(BF16),
        "w_xv": w_xv[l].astype(BF16),
        "w_xo": w_xo[l].astype(BF16),
        "g_ffn": row(g_ffn[l]),
        "w_pq_t": w_pq[l].T.astype(BF16),
        "sub_keys": sub_keys[l].astype(BF16),
        "u_table": _pack_table(u_experts, l),
        "v_table": _pack_table(v_experts, l),
    }


def kernel(x, mem, g_mix, w_in, b_if, conv_w, conv_b, g_head, w_out, g_xattn, g_mem, w_xq, w_xk, w_xv, w_xo, g_ffn, w_pq, sub_keys, u_experts, v_experts, g_final):
    batch, seq, d = x.shape
    depth = w_in.shape[0]
    nt = batch * seq
    assert d == D_MODEL and seq % XATT_TILE == 0 and seq % MIX_TILE == 0 and nt % ROUTE_TILE == 0
    x2d = x.reshape(nt, d)
    resid = None
    consts = _peer_constants()
    for l in range(depth):
        p = _layer_params(l, g_mix, w_in, b_if, conv_w, conv_b, g_head, w_out, g_xattn, g_mem,
                          w_xq, w_xk, w_xv, w_xo, g_ffn, w_pq, sub_keys, u_experts, v_experts)
        x2d = _mixer(x2d, resid, batch, seq, p)
        k, v = _kv(mem, p)
        x2d = _xattn(x2d, k, v, seq, p)
        hp4, idx, par, gate_t = _route(x2d, p)
        idx3 = idx.reshape(nt // PEER_TILE, PEER_TILE, N_SEL)
        w2 = _uside(idx3, hp4, par, gate_t, p["u_table"], consts)
        resid = _vside(idx3, w2, p["v_table"], consts)
    return _final(x2d, resid, g_final.reshape(1, d)).reshape(batch, seq, d)
```
